```python
import math
import jax, jax.numpy as jnp
from jax import lax
import numpy as np

D_MODEL = 2048
BATCH = 4
SEQ = 2048
DEPTH = 2
DEC_BATCH = 128
DEC_SEQ = 1
PAST_LEN = 16384
PAGE_SIZE = 128

N_MIXERS = 2
N_CONV_LAYERS = (DEPTH + 1) // 2
N_SSD_LAYERS = DEPTH // 2
SC_WIDTH = 3
SSD_EXPAND = 2
D_INNER = SSD_EXPAND * D_MODEL
SSD_HEAD_DIM = 64
SSD_HEADS = D_INNER // SSD_HEAD_DIM
SSD_GROUPS = 8
SSD_HPG = SSD_HEADS // SSD_GROUPS
SSD_STATE = 128
SSD_CONV_WIDTH = 4
SSD_CONV_DIM = D_INNER + 2 * SSD_GROUPS * SSD_STATE
SSD_IN_DIM = D_INNER + SSD_CONV_DIM + SSD_HEADS
SSD_CHUNK = 128
MEM_LEN = 256
MEM_HEADS = 4
MEM_HEAD_DIM = D_MODEL // MEM_HEADS
D_FF = -(-8 * D_MODEL // (3 * 256)) * 256
RMS_EPS = 1e-5

kernel_name = "hybrid_shortconv_ssd_memxattn_step"


def rmsnorm(x, g):
    xf = x.astype(jnp.float32)
    y = xf * lax.rsqrt(jnp.mean(xf * xf, axis=-1, keepdims=True) + RMS_EPS)
    return (y * g.astype(jnp.float32)).astype(x.dtype)


def causal_dwconv(u_ext, w):
    c = u_ext.shape[-1]
    return lax.conv_general_dilated(u_ext, w[:, None, :].astype(u_ext.dtype), window_strides=(1,), padding='VALID',
                                    dimension_numbers=('NWC', 'WIO', 'NWC'), feature_group_count=c)


def short_conv_mixer(h, hist, w_in, w_conv, w_out):
    b_gate, c_gate, v = jnp.split(h @ w_in, 3, axis=-1)
    u_ext = jnp.concatenate([hist.astype(h.dtype), c_gate * v], axis=1)
    y = b_gate * causal_dwconv(u_ext, w_conv)
    return y @ w_out, u_ext[:, -(SC_WIDTH - 1):]


def ssd_scan(x, dt, a, b, c, s0):
    f32 = jnp.float32
    bsz, L = x.shape[0], x.shape[1]
    q = min(SSD_CHUNK, L)
    nc = -(-L // q)
    pad = nc * q - L
    x, b, c, dt = (t.astype(f32) for t in (x, b, c, dt))
    if pad:
        padw = lambda t: jnp.pad(t, [(0, 0), (0, pad)] + [(0, 0)] * (t.ndim - 2))
        x, b, c, dt = padw(x), padw(b), padw(c), padw(dt)
    x = x.reshape(bsz, nc, q, SSD_GROUPS, SSD_HPG, SSD_HEAD_DIM)
    dt = dt.reshape(bsz, nc, q, SSD_GROUPS, SSD_HPG)
    b = b.reshape(bsz, nc, q, SSD_GROUPS, SSD_STATE)
    c = c.reshape(bsz, nc, q, SSD_GROUPS, SSD_STATE)
    cs = jnp.cumsum(dt * a.astype(f32).reshape(SSD_GROUPS, SSD_HPG), axis=2)
    tri = jnp.tril(jnp.ones((q, q), dtype=bool))[None, None, :, :, None, None]
    seg = cs[:, :, :, None] - cs[:, :, None, :]
    decay = jnp.exp(jnp.where(tri, seg, -jnp.inf))
    cb = jnp.einsum('bctgn,bcsgn->bctsg', c, b)
    w_ts = cb[..., None] * decay * dt[:, :, None]
    y_diag = jnp.einsum('bctsgj,bcsgjp->bctgjp', w_ts, x)
    to_end = jnp.exp(cs[:, :, -1:] - cs) * dt
    chunk_states = jnp.einsum('bcsgj,bcsgn,bcsgjp->bcgjpn', to_end, b, x)
    chunk_decay = jnp.exp(cs[:, :, -1])
    s_init = s0.astype(f32).reshape(bsz, SSD_GROUPS, SSD_HPG, SSD_HEAD_DIM, SSD_STATE)

    def step(s, inp):
        st, cd = inp
        return s * cd[..., None, None] + st, s

    s_final, s_in = lax.scan(step, s_init, (jnp.swapaxes(chunk_states, 0, 1), jnp.swapaxes(chunk_decay, 0, 1)))
    s_in = jnp.swapaxes(s_in, 0, 1)
    y_off = jnp.einsum('bctgn,bcgjpn,bctgj->bctgjp', c, s_in, jnp.exp(cs))
    y = (y_diag + y_off).reshape(bsz, nc * q, SSD_HEADS, SSD_HEAD_DIM)[:, :L]
    return y, s_final.reshape(bsz, SSD_HEADS, SSD_HEAD_DIM, SSD_STATE)


def ssd_mixer(h, conv_hist, state, w_in, conv_w, conv_b, dt_bias, a_log, d_skip, norm_g, w_out):
    bsz, L, _ = h.shape
    z, xbc, dt_raw = jnp.split(h @ w_in, [D_INNER, D_INNER + SSD_CONV_DIM], axis=-1)
    xbc_ext = jnp.concatenate([conv_hist.astype(h.dtype), xbc], axis=1)
    xbc_c = jax.nn.silu(causal_dwconv(xbc_ext, conv_w) + conv_b.astype(h.dtype))
    xs, b_in, c_in = jnp.split(xbc_c, [D_INNER, D_INNER + SSD_GROUPS * SSD_STATE], axis=-1)
    xs = xs.reshape(bsz, L, SSD_HEADS, SSD_HEAD_DIM)
    b_in = b_in.reshape(bsz, L, SSD_GROUPS, SSD_STATE)
    c_in = c_in.reshape(bsz, L, SSD_GROUPS, SSD_STATE)
    dt = jax.nn.softplus(dt_raw.astype(jnp.float32) + dt_bias.astype(jnp.float32))
    a = -jnp.exp(a_log.astype(jnp.float32))
    y, new_state = ssd_scan(xs, dt, a, b_in, c_in, state)
    y = y + d_skip.astype(jnp.float32)[:, None] * xs.astype(jnp.float32)
    g = (y.reshape(bsz, L, D_INNER) * jax.nn.silu(z.astype(jnp.float32))).reshape(bsz, L, SSD_GROUPS, D_INNER // SSD_GROUPS)
    g = rmsnorm(g, norm_g.reshape(SSD_GROUPS, D_INNER // SSD_GROUPS)).reshape(bsz, L, D_INNER).astype(h.dtype)
    return g @ w_out, xbc_ext[:, -(SSD_CONV_WIDTH - 1):], new_state.astype(state.dtype)


def mem_project(mem, g, w_k, w_v):
    bsz = mem.shape[0]
    m = rmsnorm(mem, g)
    k = (m @ w_k).reshape(bsz, MEM_LEN, MEM_HEADS, MEM_HEAD_DIM)
    v = (m @ w_v).reshape(bsz, MEM_LEN, MEM_HEADS, MEM_HEAD_DIM)
    return k, v


def mem_attend(h, k, v, w_q, w_o):
    bsz, L, _ = h.shape
    q = (h @ w_q).reshape(bsz, L, MEM_HEADS, MEM_HEAD_DIM)
    s = jnp.einsum('blhd,bmhd->bhlm', q, k.astype(q.dtype)).astype(jnp.float32) * (MEM_HEAD_DIM ** -0.5)
    p = jax.nn.softmax(s, axis=-1).astype(h.dtype)
    o = jnp.einsum('bhlm,bmhd->blhd', p, v.astype(h.dtype)).reshape(bsz, L, D_MODEL)
    return o @ w_o


def swiglu(h, w_gate, w_up, w_down):
    return (jax.nn.silu(h @ w_gate) * (h @ w_up)) @ w_down


def trunk(x, sc_hist, ssd_conv_hist, ssd_state, mem_k, mem_v, p):
    new_sc, new_sconv, new_ss = [], [], []
    h = x
    for i in range(DEPTH):
        hn = rmsnorm(h, p['norm_mix'][i])
        if i % N_MIXERS == 0:
            a = i // N_MIXERS
            out, hist = short_conv_mixer(hn, sc_hist[a], p['sc_w_in'][a], p['sc_w_conv'][a], p['sc_w_out'][a])
            new_sc.append(hist)
        else:
            j = i // N_MIXERS
            out, chist, st = ssd_mixer(hn, ssd_conv_hist[j], ssd_state[j], p['ssd_w_in'][j], p['ssd_conv_w'][j],
                                       p['ssd_conv_b'][j], p['ssd_dt_bias'][j], p['ssd_a_log'][j], p['ssd_d'][j],
                                       p['ssd_norm'][j], p['ssd_w_out'][j])
            new_sconv.append(chist)
            new_ss.append(st)
        h = h + out
        h = h + mem_attend(rmsnorm(h, p['norm_mem_q'][i]), mem_k[i], mem_v[i], p['xa_w_q'][i], p['xa_w_o'][i])
        h = h + swiglu(rmsnorm(h, p['norm_ffn'][i]), p['ffn_w_gate'][i], p['ffn_w_up'][i], p['ffn_w_down'][i])
    return rmsnorm(h, p['norm_final']), jnp.stack(new_sc), jnp.stack(new_sconv), jnp.stack(new_ss)


def setup_inputs(seed: int = 0) -> dict:
    key = jax.random.key(seed)
    ks = iter(jax.random.split(key, 64))
    f32 = jnp.float32

    def nrm(shape, scale):
        return jax.random.normal(next(ks), shape, f32) * scale

    def gain(shape):
        return 1.0 + nrm(shape, 0.02)

    n_a, n_b = N_CONV_LAYERS, N_SSD_LAYERS
    dt0 = jnp.exp(jax.random.uniform(next(ks), (n_b, SSD_HEADS), f32, math.log(1e-3), math.log(1e-1)))
    dt_bias = dt0 + jnp.log(-jnp.expm1(-dt0))
    a_log = jnp.log(jax.random.uniform(next(ks), (n_b, SSD_HEADS), f32, 1.0, 16.0))
    sd = D_MODEL ** -0.5
    return {
        "x_prompt": nrm((BATCH, SEQ, D_MODEL), 1.0),
        "x_sample": nrm((DEC_BATCH, DEC_SEQ, D_MODEL), 1.0),
        "mem_prompt": nrm((BATCH, MEM_LEN, D_MODEL), 1.0),
        "cache_sc": nrm((n_a, DEC_BATCH, SC_WIDTH - 1, D_MODEL), 1.0),
        "state_ssd_conv": nrm((n_b, DEC_BATCH, SSD_CONV_WIDTH - 1, SSD_CONV_DIM), 1.0),
        "state_ssd": nrm((n_b, DEC_BATCH, SSD_HEADS, SSD_HEAD_DIM, SSD_STATE), 0.1),
        "cache_mem_k": nrm((DEPTH, DEC_BATCH, MEM_LEN, MEM_HEADS, MEM_HEAD_DIM), 1.0),
        "cache_mem_v": nrm((DEPTH, DEC_BATCH, MEM_LEN, MEM_HEADS, MEM_HEAD_DIM), 1.0),
        "norm_mix": gain((DEPTH, D_MODEL)),
        "norm_mem_q": gain((DEPTH, D_MODEL)),
        "norm_mem_kv": gain((DEPTH, D_MODEL)),
        "norm_ffn": gain((DEPTH, D_MODEL)),
        "norm_final": gain((D_MODEL,)),
        "sc_w_in": nrm((n_a, D_MODEL, 3 * D_MODEL), sd),
        "sc_w_conv": nrm((n_a, SC_WIDTH, D_MODEL), SC_WIDTH ** -0.5),
        "sc_w_out": nrm((n_a, D_MODEL, D_MODEL), sd),
        "ssd_w_in": nrm((n_b, D_MODEL, SSD_IN_DIM), sd),
        "ssd_conv_w": nrm((n_b, SSD_CONV_WIDTH, SSD_CONV_DIM), SSD_CONV_WIDTH ** -0.5),
        "ssd_conv_b": nrm((n_b, SSD_CONV_DIM), 0.01),
        "ssd_dt_bias": dt_bias,
        "ssd_a_log": a_log,
        "ssd_d": 1.0 + nrm((n_b, SSD_HEADS), 0.1),
        "ssd_norm": gain((n_b, D_INNER)),
        "ssd_w_out": nrm((n_b, D_INNER, D_MODEL), D_INNER ** -0.5),
        "xa_w_q": nrm((DEPTH, D_MODEL, D_MODEL), sd),
        "xa_w_k": nrm((DEPTH, D_MODEL, D_MODEL), sd),
        "xa_w_v": nrm((DEPTH, D_MODEL, D_MODEL), sd),
        "xa_w_o": nrm((DEPTH, D_MODEL, D_MODEL), sd),
        "ffn_w_gate": nrm((DEPTH, D_MODEL, D_FF), sd),
        "ffn_w_up": nrm((DEPTH, D_MODEL, D_FF), sd),
        "ffn_w_down": nrm((DEPTH, D_FF, D_MODEL), D_FF ** -0.5),
    }


def reference(x_prompt, x_sample, mem_prompt, cache_sc, state_ssd_conv, state_ssd, cache_mem_k, cache_mem_v,
              norm_mix, norm_mem_q, norm_mem_kv, norm_ffn, norm_final,
              sc_w_in, sc_w_conv, sc_w_out,
              ssd_w_in, ssd_conv_w, ssd_conv_b, ssd_dt_bias, ssd_a_log, ssd_d, ssd_norm, ssd_w_out,
              xa_w_q, xa_w_k, xa_w_v, xa_w_o, ffn_w_gate, ffn_w_up, ffn_w_down):
    p = {'norm_mix': norm_mix, 'norm_mem_q': norm_mem_q, 'norm_ffn': norm_ffn, 'norm_final': norm_final,
         'sc_w_in': sc_w_in, 'sc_w_conv': sc_w_conv, 'sc_w_out': sc_w_out,
         'ssd_w_in': ssd_w_in, 'ssd_conv_w': ssd_conv_w, 'ssd_conv_b': ssd_conv_b, 'ssd_dt_bias': ssd_dt_bias,
         'ssd_a_log': ssd_a_log, 'ssd_d': ssd_d, 'ssd_norm': ssd_norm, 'ssd_w_out': ssd_w_out,
         'xa_w_q': xa_w_q, 'xa_w_o': xa_w_o, 'ffn_w_gate': ffn_w_gate, 'ffn_w_up': ffn_w_up, 'ffn_w_down': ffn_w_down}
    dt = x_prompt.dtype
    kv = [mem_project(mem_prompt, norm_mem_kv[i], xa_w_k[i], xa_w_v[i]) for i in range(DEPTH)]
    mem_k_p = jnp.stack([k for k, _ in kv])
    mem_v_p = jnp.stack([v for _, v in kv])
    zero_sc = jnp.zeros((N_CONV_LAYERS, BATCH, SC_WIDTH - 1, D_MODEL), dt)
    zero_sconv = jnp.zeros((N_SSD_LAYERS, BATCH, SSD_CONV_WIDTH - 1, SSD_CONV_DIM), dt)
    zero_ss = jnp.zeros((N_SSD_LAYERS, BATCH, SSD_HEADS, SSD_HEAD_DIM, SSD_STATE), dt)
    y_prompt, sc_p, sconv_p, ss_p = trunk(x_prompt, zero_sc, zero_sconv, zero_ss, mem_k_p, mem_v_p, p)
    y_sample, sc_s, sconv_s, ss_s = trunk(x_sample, cache_sc, state_ssd_conv, state_ssd, cache_mem_k, cache_mem_v, p)
    return (y_prompt, y_sample, sc_p, sc_s, sconv_p, sconv_s, ss_p, ss_s, mem_k_p, mem_v_p)
```

```python
import functools

import jax
import jax.numpy as jnp
from jax import lax
from jax.experimental import pallas as pl
from jax.experimental.pallas import tpu as pltpu

F32 = jnp.float32
BF16 = jnp.bfloat16

D_MODEL = 2048
BATCH = 4
SEQ = 2048
DEC_BATCH = 128
T_P = BATCH * SEQ
T_S = DEC_BATCH
T_ALL = T_P + T_S
D_INNER = 4096
SSD_HEAD_DIM = 64
SSD_HEADS = 64
SSD_GROUPS = 8
SSD_HPG = 8
SSD_STATE = 128
GROUP_W = SSD_HPG * SSD_HEAD_DIM
SSD_CONV_DIM = D_INNER + 2 * SSD_GROUPS * SSD_STATE
MEM_LEN = 256
MEM_HEADS = 4
MEM_HEAD_DIM = 512
D_FF = 5632
RMS_EPS = 1e-5

V7X_VMEM_LIMIT_BYTES = 56 * 1024 * 1024
TOKEN_TILE = 640
SSD_CHUNK = 128


def _cparams(*sem):
    return pltpu.CompilerParams(dimension_semantics=sem, vmem_limit_bytes=V7X_VMEM_LIMIT_BYTES)


def _rmsnorm_kernel(x_ref, g_ref, o_ref):
    x = x_ref[...]
    ms = jnp.mean(x * x, axis=-1, keepdims=True)
    o_ref[...] = (x * lax.rsqrt(ms + RMS_EPS) * g_ref[...]).astype(o_ref.dtype)


def _rmsnorm(x, g, out_dtype, *, tm, row_block0=0, rows=None):
    rows = x.shape[0] if rows is None else rows
    d = x.shape[1]
    return pl.pallas_call(
        _rmsnorm_kernel,
        grid=(rows // tm,),
        in_specs=[pl.BlockSpec((tm, d), lambda i: (i + row_block0, 0)),
                  pl.BlockSpec((1, d), lambda i: (0, 0))],
        out_specs=pl.BlockSpec((tm, d), lambda i: (i, 0)),
        out_shape=jax.ShapeDtypeStruct((rows, d), out_dtype),
        compiler_params=_cparams("arbitrary"),
        name="rmsnorm",
    )(x, g.reshape(1, d).astype(F32))


def _mm_kernel(x_ref, w_ref, *rest, has_res):
    if has_res:
        r_ref, o_ref, wb_ref = rest
    else:
        o_ref, wb_ref = rest

    @pl.when(pl.program_id(1) == 0)
    def _():
        wb_ref[...] = w_ref[...].astype(BF16)

    acc = jnp.dot(x_ref[...], wb_ref[...], preferred_element_type=F32)
    if has_res:
        acc = acc + r_ref[...]
    o_ref[...] = acc.astype(o_ref.dtype)


def _matmul(x, w3, layer, *, tn, tm, n_out=None, res=None, out_dtype=F32):
    m, k = x.shape
    n = w3.shape[2] if n_out is None else n_out
    in_specs = [pl.BlockSpec((tm, k), lambda j, i: (i, 0)),
                pl.BlockSpec((None, k, tn), lambda j, i: (layer, 0, j))]
    args = [x, w3]
    if res is not None:
        in_specs.append(pl.BlockSpec((tm, tn), lambda j, i: (i, j)))
        args.append(res)
    return pl.pallas_call(
        functools.partial(_mm_kernel, has_res=res is not None),
        grid=(n // tn, m // tm),
        in_specs=in_specs,
        out_specs=pl.BlockSpec((tm, tn), lambda j, i: (i, j)),
        out_shape=jax.ShapeDtypeStruct((m, n), out_dtype),
        scratch_shapes=[pltpu.VMEM((k, tn), BF16)],
        compiler_params=_cparams("arbitrary", "arbitrary"),
        name="matmul",
    )(*args)


def _swiglu_kernel(x_ref, wg_ref, wu_ref, o_ref, wgb_ref, wub_ref):
    @pl.when(pl.program_id(1) == 0)
    def _():
        wgb_ref[...] = wg_ref[...].astype(BF16)
        wub_ref[...] = wu_ref[...].astype(BF16)

    x = x_ref[...]
    a = jnp.dot(x, wgb_ref[...], preferred_element_type=F32)
    u = jnp.dot(x, wub_ref[...], preferred_element_type=F32)
    o_ref[...] = (a * jax.nn.sigmoid(a) * u).astype(o_ref.dtype)


def _swiglu_up(x, wg3, wu3, layer, *, tn, tm):
    m, k = x.shape
    n = wg3.shape[2]
    wspec = pl.BlockSpec((None, k, tn), lambda j, i: (layer, 0, j))
    return pl.pallas_call(
        _swiglu_kernel,
        grid=(n // tn, m // tm),
        in_specs=[pl.BlockSpec((tm, k), lambda j, i: (i, 0)), wspec, wspec],
        out_specs=pl.BlockSpec((tm, tn), lambda j, i: (i, j)),
        out_shape=jax.ShapeDtypeStruct((m, n), BF16),
        scratch_shapes=[pltpu.VMEM((k, tn), BF16), pltpu.VMEM((k, tn), BF16)],
        compiler_params=_cparams("arbitrary", "arbitrary"),
        name="swiglu_up",
    )(x, wg3, wu3)


def _shift_rows(u, k, row):
    return jnp.where(row >= k, pltpu.roll(u, k, axis=0), 0.0)


def _sc_prompt_kernel(bg_ref, cg_ref, v_ref, w_ref, y_ref, hist_ref):
    u = cg_ref[...] * v_ref[...]
    row = lax.broadcasted_iota(jnp.int32, u.shape, 0)
    w = w_ref[...]
    conv = w[2:3, :] * u + w[1:2, :] * _shift_rows(u, 1, row) + w[0:1, :] * _shift_rows(u, 2, row)
    y_ref[...] = (bg_ref[...] * conv).astype(y_ref.dtype)
    hist_ref[...] = u[SEQ - 2:, :]


def _sc_prompt(bcv, w_conv, *, tc=256):
    nb = D_MODEL // tc
    return pl.pallas_call(
        _sc_prompt_kernel,
        grid=(BATCH, nb),
        in_specs=[pl.BlockSpec((SEQ, tc), lambda b, j: (b, j)),
                  pl.BlockSpec((SEQ, tc), lambda b, j: (b, j + nb)),
                  pl.BlockSpec((SEQ, tc), lambda b, j: (b, j + 2 * nb)),
                  pl.BlockSpec((3, tc), lambda b, j: (0, j))],
        out_specs=[pl.BlockSpec((SEQ, tc), lambda b, j: (b, j)),
                   pl.BlockSpec((None, 2, tc), lambda b, j: (b, 0, j))],
        out_shape=[jax.ShapeDtypeStruct((T_P, D_MODEL), BF16),
                   jax.ShapeDtypeStruct((BATCH, 2, D_MODEL), F32)],
        compiler_params=_cparams("arbitrary", "arbitrary"),
        name="sc_prompt",
    )(bcv, bcv, bcv, w_conv)


def _sc_sample_kernel(bg_ref, cg_ref, v_ref, h0_ref, h1_ref, w_ref, y_ref, u_ref):
    u = cg_ref[...] * v_ref[...]
    w = w_ref[...]
    conv = w[2:3, :] * u + w[1:2, :] * h1_ref[...] + w[0:1, :] * h0_ref[...]
    y_ref[...] = (bg_ref[...] * conv).astype(y_ref.dtype)
    u_ref[...] = u


def _sc_sample(bcv, h0, h1, w_conv, *, tc=512):
    nb = D_MODEL // tc
    rb = T_P // T_S
    hspec = pl.BlockSpec((T_S, tc), lambda j: (0, j))
    return pl.pallas_call(
        _sc_sample_kernel,
        grid=(nb,),
        in_specs=[pl.BlockSpec((T_S, tc), lambda j: (rb, j)),
                  pl.BlockSpec((T_S, tc), lambda j: (rb, j + nb)),
                  pl.BlockSpec((T_S, tc), lambda j: (rb, j + 2 * nb)),
                  hspec, hspec,
                  pl.BlockSpec((3, tc), lambda j: (0, j))],
        out_specs=[hspec, hspec],
        out_shape=[jax.ShapeDtypeStruct((T_S, D_MODEL), BF16),
                   jax.ShapeDtypeStruct((T_S, D_MODEL), F32)],
        compiler_params=_cparams("arbitrary"),
        name="sc_sample",
    )(bcv, bcv, bcv, h0, h1, w_conv)


def _silu(x):
    return x * jax.nn.sigmoid(x)


def _ssdconv_prompt_kernel(x_ref, w_ref, b_ref, o_ref, hist_ref):
    x = x_ref[...]
    row = lax.broadcasted_iota(jnp.int32, x.shape, 0)
    w = w_ref[...]
    conv = (w[3:4, :] * x + w[2:3, :] * _shift_rows(x, 1, row)
            + w[1:2, :] * _shift_rows(x, 2, row) + w[0:1, :] * _shift_rows(x, 3, row))
    o_ref[...] = _silu(conv + b_ref[...])
    hist_ref[...] = x[SEQ - 3:, :]


def _ssdconv_prompt(zx, conv_w, conv_b, *, tc=256):
    nb = SSD_CONV_DIM // tc
    off = D_INNER // tc
    return pl.pallas_call(
        _ssdconv_prompt_kernel,
        grid=(BATCH, nb),
        in_specs=[pl.BlockSpec((SEQ, tc), lambda b, j: (b, j + off)),
                  pl.BlockSpec((4, tc), lambda b, j: (0, j)),
                  pl.BlockSpec((1, tc), lambda b, j: (0, j))],
        out_specs=[pl.BlockSpec((SEQ, tc), lambda b, j: (b, j)),
                   pl.BlockSpec((None, 3, tc), lambda b, j: (b, 0, j))],
        out_shape=[jax.ShapeDtypeStruct((T_P, SSD_CONV_DIM), F32),
                   jax.ShapeDtypeStruct((BATCH, 3, SSD_CONV_DIM), F32)],
        compiler_params=_cparams("arbitrary", "arbitrary"),
        name="ssdconv_prompt",
    )(zx, conv_w, conv_b)


def _ssdconv_sample_kernel(x_ref, h0_ref, h1_ref, h2_ref, w_ref, b_ref, o_ref, raw_ref):
    x = x_ref[...]
    w = w_ref[...]
    conv = w[3:4, :] * x + w[2:3, :] * h2_ref[...] + w[1:2, :] * h1_ref[...] + w[0:1, :] * h0_ref[...]
    o_ref[...] = _silu(conv + b_ref[...])
    raw_ref[...] = x


def _ssdconv_sample(zx, h0, h1, h2, conv_w, conv_b, *, tc=512):
    nb = SSD_CONV_DIM // tc
    off = D_INNER // tc
    rb = T_P // T_S
    hspec = pl.BlockSpec((T_S, tc), lambda j: (0, j))
    return pl.pallas_call(
        _ssdconv_sample_kernel,
        grid=(nb,),
        in_specs=[pl.BlockSpec((T_S, tc), lambda j: (rb, j + off)), hspec, hspec, hspec,
                  pl.BlockSpec((4, tc), lambda j: (0, j)),
                  pl.BlockSpec((1, tc), lambda j: (0, j))],
        out_specs=[hspec, hspec],
        out_shape=[jax.ShapeDtypeStruct((T_S, SSD_CONV_DIM), F32),
                   jax.ShapeDtypeStruct((T_S, SSD_CONV_DIM), F32)],
        compiler_params=_cparams("arbitrary"),
        name="ssdconv_sample",
    )(zx, h0, h1, h2, conv_w, conv_b)


def _dt_kernel(x_ref, w_ref, bias_ref, alog_ref, dt_ref, dec_ref, cs_ref):
    raw = jnp.dot(x_ref[...], w_ref[...], preferred_element_type=F32) + bias_ref[...]
    dt = jnp.maximum(raw, 0.0) + jnp.log1p(jnp.exp(-jnp.abs(raw)))
    da = dt * (-jnp.exp(alog_ref[...]))
    dt_ref[...] = dt
    dec_ref[...] = jnp.exp(da)
    row = lax.broadcasted_iota(jnp.int32, da.shape, 0)
    cs = da
    k = 1
    while k < SSD_CHUNK:
        cs = cs + jnp.where(row >= k, pltpu.roll(cs, k, axis=0), 0.0)
        k *= 2
    cs_ref[...] = cs


def _dt_prep(hn, w_dt, dt_bias, a_log):
    k = hn.shape[1]
    ospec = pl.BlockSpec((SSD_CHUNK, 128), lambda i: (i, 0))
    oshape = jax.ShapeDtypeStruct((T_ALL, 128), F32)
    return pl.pallas_call(
        _dt_kernel,
        grid=(T_ALL // SSD_CHUNK,),
        in_specs=[pl.BlockSpec((SSD_CHUNK, k), lambda i: (i, 0)),
                  pl.BlockSpec((k, 128), lambda i: (0, 0)),
                  pl.BlockSpec((1, 128), lambda i: (0, 0)),
                  pl.BlockSpec((1, 128), lambda i: (0, 0))],
        out_specs=[ospec, ospec, ospec],
        out_shape=[oshape, oshape, oshape],
        compiler_params=_cparams("arbitrary"),
        name="ssd_dt",
    )(hn, w_dt, dt_bias, a_log)


def _expand_heads(v, onehot):
    hi = v.astype(BF16)
    lo = (v - hi.astype(F32)).astype(BF16)
    return (jnp.dot(hi, onehot, preferred_element_type=F32)
            + jnp.dot(lo, onehot, preferred_element_type=F32))


def _gate_norm(y, z, ng):
    g = y * _silu(z)
    ms = jnp.mean(g * g, axis=-1, keepdims=True)
    return g * lax.rsqrt(ms + RMS_EPS) * ng


def _ssd_scan_kernel(x_ref, b_ref, c_ref, z_ref, dtc_ref, csc_ref, dtr_ref, csr_ref, dsk_ref, ng_ref,
                     y_ref, sfin_ref, st_ref):
    q = SSD_CHUNK
    c = pl.program_id(2)

    @pl.when(c == 0)
    def _():
        st_ref[...] = jnp.zeros_like(st_ref)

    x = x_ref[...]
    bm = b_ref[...]
    xb = x.astype(BF16)
    bb = bm.astype(BF16)
    cb = c_ref[...].astype(BF16)
    csc = csc_ref[...]
    dtc = dtc_ref[...]
    csr = csr_ref[...]
    dtr = dtr_ref[...]
    st = st_ref[...]

    lane = lax.broadcasted_iota(jnp.int32, (SSD_HPG, GROUP_W), 1)
    head = lax.broadcasted_iota(jnp.int32, (SSD_HPG, GROUP_W), 0)
    onehot = jnp.where(lane // SSD_HEAD_DIM == head, 1.0, 0.0).astype(BF16)

    cbm = lax.dot_general(cb, bb, (((1,), (1,)), ((), ())), preferred_element_type=F32)
    row = lax.broadcasted_iota(jnp.int32, (q, q), 0)
    col = lax.broadcasted_iota(jnp.int32, (q, q), 1)
    tri = row >= col
    lane128 = lax.broadcasted_iota(jnp.int32, (q, 128), 1)
    lo_half = lane128 < SSD_HEAD_DIM

    tiles = []
    for m in range(GROUP_W // 128):
        xt = xb[:, 128 * m:128 * (m + 1)]
        acc = None
        for j, xm in ((2 * m, jnp.where(lo_half, xt, 0)), (2 * m + 1, jnp.where(lo_half, 0, xt))):
            seg = csc[:, j:j + 1] - csr[j:j + 1, :]
            dec = jnp.exp(jnp.where(tri, seg, -jnp.inf))
            w = (cbm * dec * dtr[j:j + 1, :]).astype(BF16)
            part = jnp.dot(w, xm, preferred_element_type=F32)
            acc = part if acc is None else acc + part
        tiles.append(acc)
    y_diag = jnp.concatenate(tiles, axis=1)

    e_exp = _expand_heads(jnp.exp(csc), onehot)
    y_off = jnp.dot(cb, st.astype(BF16), preferred_element_type=F32) * e_exp

    cs_last = csc[q - 1:q, :]
    te = _expand_heads(jnp.exp(cs_last - csc) * dtc, onehot)
    xw = (x * te).astype(BF16)
    d_st = jnp.dot(bm.T.astype(BF16), xw, preferred_element_type=F32)
    st_new = st * e_exp[q - 1:q, :] + d_st
    st_ref[...] = st_new

    y = y_diag + y_off + dsk_ref[...] * x
    y_ref[...] = _gate_norm(y, z_ref[...], ng_ref[...]).astype(y_ref.dtype)

    @pl.when(c == pl.num_programs(2) - 1)
    def _():
        sfin_ref[...] = st_new.T


def _ssd_scan_prompt(xbc_c, zx, dtc, csc, dtr, csr, dskip, ng):
    q = SSD_CHUNK
    nc = SEQ // q
    xoff = D_INNER // SSD_STATE
    small_c = pl.BlockSpec((None, q, SSD_HPG), lambda b, g, c: (g, b * nc + c, 0))
    small_r = pl.BlockSpec((None, SSD_HPG, q), lambda b, g, c: (g, 0, b * nc + c))
    pspec = pl.BlockSpec((None, 1, GROUP_W), lambda b, g, c: (g, 0, 0))
    return pl.pallas_call(
        _ssd_scan_kernel,
        grid=(BATCH, SSD_GROUPS, nc),
        in_specs=[pl.BlockSpec((q, GROUP_W), lambda b, g, c: (b * nc + c, g)),
                  pl.BlockSpec((q, SSD_STATE), lambda b, g, c: (b * nc + c, xoff + g)),
                  pl.BlockSpec((q, SSD_STATE), lambda b, g, c: (b * nc + c, xoff + SSD_GROUPS + g)),
                  pl.BlockSpec((q, GROUP_W), lambda b, g, c: (b * nc + c, g)),
                  small_c, small_c, small_r, small_r, pspec, pspec],
        out_specs=[pl.BlockSpec((q, GROUP_W), lambda b, g, c: (b * nc + c, g)),
                   pl.BlockSpec((GROUP_W, SSD_STATE), lambda b, g, c: (b * SSD_GROUPS + g, 0))],
        out_shape=[jax.ShapeDtypeStruct((T_P, D_INNER), BF16),
                   jax.ShapeDtypeStruct((BATCH * D_INNER, SSD_STATE), F32)],
        scratch_shapes=[pltpu.VMEM((SSD_STATE, GROUP_W), F32)],
        compiler_params=_cparams("arbitrary", "arbitrary", "arbitrary"),
        name="ssd_scan",
    )(xbc_c, xbc_c, xbc_c, zx, dtc, csc, dtr, csr, dskip, ng)


def _ssd_step_kernel(s_ref, xt_ref, dtt_ref, b_ref, c_ref, dec_ref, so_ref, yt_ref):
    xdt = xt_ref[...] * dtt_ref[...]
    for r in range(D_INNER // 128):
        g = r // (GROUP_W // 128)
        s_t = s_ref[128 * r:128 * (r + 1), :]
        brow = b_ref[:, SSD_STATE * g:SSD_STATE * (g + 1)]
        crow = c_ref[:, SSD_STATE * g:SSD_STATE * (g + 1)]
        dtile = jnp.concatenate(
            [jnp.broadcast_to(dec_ref[2 * r:2 * r + 1, :], (SSD_HEAD_DIM, SSD_STATE)),
             jnp.broadcast_to(dec_ref[2 * r + 1:2 * r + 2, :], (SSD_HEAD_DIM, SSD_STATE))], axis=0)
        s_n = s_t * dtile + xdt[:, r:r + 1] * brow
        so_ref[128 * r:128 * (r + 1), :] = s_n
        yt_ref[:, r:r + 1] = jnp.sum(s_n * crow, axis=-1, keepdims=True)


def _ssd_step_sample(state, x_t, dt_t, b_s, c_s, dec):
    nr = D_INNER // 128
    tspec = pl.BlockSpec((None, 128, nr), lambda b: (b, 0, 0))
    rspec = pl.BlockSpec((None, 1, SSD_GROUPS * SSD_STATE), lambda b: (b, 0, 0))
    sspec = pl.BlockSpec((None, D_INNER, SSD_STATE), lambda b: (b, 0, 0))
    return pl.pallas_call(
        _ssd_step_kernel,
        grid=(DEC_BATCH,),
        in_specs=[sspec, tspec, tspec, rspec, rspec,
                  pl.BlockSpec((None, SSD_HEADS, SSD_STATE), lambda b: (b, 0, 0))],
        out_specs=[sspec, tspec],
        out_shape=[jax.ShapeDtypeStruct((DEC_BATCH, D_INNER, SSD_STATE), F32),
                   jax.ShapeDtypeStruct((DEC_BATCH, 128, nr), F32)],
        compiler_params=_cparams("arbitrary"),
        name="ssd_step",
    )(state, x_t, dt_t, b_s, c_s, dec)


def _gate_norm_kernel(y_ref, x_ref, z_ref, dsk_ref, ng_ref, o_ref):
    y = y_ref[...] + dsk_ref[...] * x_ref[...]
    o_ref[...] = _gate_norm(y, z_ref[...], ng_ref[...]).astype(o_ref.dtype)


def _gate_norm_sample(y_s, xbc_c_s, zx, dskip, ng):
    rb = T_P // T_S
    gspec = pl.BlockSpec((T_S, GROUP_W), lambda g: (0, g))
    pspec = pl.BlockSpec((None, 1, GROUP_W), lambda g: (g, 0, 0))
    return pl.pallas_call(
        _gate_norm_kernel,
        grid=(SSD_GROUPS,),
        in_specs=[gspec, gspec, pl.BlockSpec((T_S, GROUP_W), lambda g: (rb, g)), pspec, pspec],
        out_specs=gspec,
        out_shape=jax.ShapeDtypeStruct((T_S, D_INNER), BF16),
        compiler_params=_cparams("arbitrary"),
        name="ssd_gate_norm",
    )(y_s, xbc_c_s, zx, dskip, ng)


def _softmax_rows(s):
    m = jnp.max(s, axis=-1, keepdims=True)
    e = jnp.exp(s - m)
    return e / jnp.sum(e, axis=-1, keepdims=True)


def _xattn_prompt_kernel(q_ref, k_ref, v_ref, o_ref):
    scale = MEM_HEAD_DIM ** -0.5
    for h in range(MEM_HEADS):
        sl = slice(MEM_HEAD_DIM * h, MEM_HEAD_DIM * (h + 1))
        kh = k_ref[:, sl].astype(BF16)
        vh = v_ref[:, sl].astype(BF16)
        s = lax.dot_general(q_ref[:, sl], kh, (((1,), (1,)), ((), ())), preferred_element_type=F32)
        p = _softmax_rows(s * scale).astype(BF16)
        o_ref[:, sl] = jnp.dot(p, vh, preferred_element_type=F32).astype(o_ref.dtype)


def _xattn_prompt(q, k, v, *, tq=512):
    nq = SEQ // tq
    kspec = pl.BlockSpec((MEM_LEN, D_MODEL), lambda b, i: (b, 0))
    return pl.pallas_call(
        _xattn_prompt_kernel,
        grid=(BATCH, nq),
        in_specs=[pl.BlockSpec((tq, D_MODEL), lambda b, i: (b * nq + i, 0)), kspec, kspec],
        out_specs=pl.BlockSpec((tq, D_MODEL), lambda b, i: (b * nq + i, 0)),
        out_shape=jax.ShapeDtypeStruct((T_P, D_MODEL), BF16),
        compiler_params=_cparams("arbitrary", "arbitrary"),
        name="xattn_prompt",
    )(q, k, v)


def _xattn_sample_kernel(q_ref, k_ref, v_ref, o_ref, *, bb):
    scale = MEM_HEAD_DIM ** -0.5
    for b in range(bb):
        prod = k_ref[b] * q_ref[b]
        for h in range(MEM_HEADS):
            sl = slice(MEM_HEAD_DIM * h, MEM_HEAD_DIM * (h + 1))
            s = jnp.sum(prod[:, sl], axis=-1, keepdims=True) * scale
            m = jnp.max(s, axis=0, keepdims=True)
            e = jnp.exp(s - m)
            p = e / jnp.sum(e, axis=0, keepdims=True)
            o_ref[b, :, sl] = jnp.sum(p * v_ref[b, :, sl], axis=0, keepdims=True)


def _xattn_sample(q_s, kc, vc, layer, *, bb=2):
    cspec = pl.BlockSpec((None, bb, MEM_LEN, D_MODEL), lambda i: (layer, i, 0, 0))
    qspec = pl.BlockSpec((bb, 1, D_MODEL), lambda i: (i, 0, 0))
    return pl.pallas_call(
        functools.partial(_xattn_sample_kernel, bb=bb),
        grid=(DEC_BATCH // bb,),
        in_specs=[qspec, cspec, cspec],
        out_specs=qspec,
        out_shape=jax.ShapeDtypeStruct((DEC_BATCH, 1, D_MODEL), F32),
        compiler_params=_cparams("arbitrary"),
        name="xattn_sample",
    )(q_s, kc, vc)


def _mem_attn_block(h, i, norm_mem_q, xa_w_q, xa_w_o, k_p, v_p, kc, vc):
    hq = _rmsnorm(h, norm_mem_q[i], BF16, tm=TOKEN_TILE)
    q = _matmul(hq, xa_w_q, i, tn=1024, tm=TOKEN_TILE, out_dtype=BF16)
    o_p = _xattn_prompt(q, k_p, v_p)
    q_s = q[T_P:].astype(F32).reshape(DEC_BATCH, 1, D_MODEL)
    o_s = _xattn_sample(q_s, kc, vc, i).reshape(DEC_BATCH, D_MODEL).astype(BF16)
    o = jnp.concatenate([o_p, o_s], axis=0)
    return _matmul(o, xa_w_o, i, tn=1024, tm=TOKEN_TILE, res=h)


def _ffn_block(h, i, norm_ffn, w_gate, w_up, w_down):
    hf = _rmsnorm(h, norm_ffn[i], BF16, tm=TOKEN_TILE)
    a = _swiglu_up(hf, w_gate, w_up, i, tn=512, tm=TOKEN_TILE)
    return _matmul(a, w_down, i, tn=512, tm=TOKEN_TILE, res=h)


def _short_conv_block(h, a, norm_g, cache_sc, sc_w_in, sc_w_conv, sc_w_out):
    hn = _rmsnorm(h, norm_g, BF16, tm=TOKEN_TILE)
    bcv = _matmul(hn, sc_w_in, a, tn=1024, tm=TOKEN_TILE)
    w_conv = sc_w_conv[a]
    y_p, hist_p = _sc_prompt(bcv, w_conv)
    h0, h1 = cache_sc[a, :, 0, :], cache_sc[a, :, 1, :]
    y_s, u_s = _sc_sample(bcv, h0, h1, w_conv)
    y = jnp.concatenate([y_p, y_s], axis=0)
    h = _matmul(y, sc_w_out, a, tn=1024, tm=TOKEN_TILE, res=h)
    hist_s = jnp.stack([h1, u_s], axis=1)
    return h, hist_p, hist_s


def _ssd_block(h, j, norm_g, conv_hist, state, ssd_w_in, conv_w, conv_b, dt_bias, a_log, d_skip, norm_ssd, w_out):
    q = SSD_CHUNK
    hn = _rmsnorm(h, norm_g, BF16, tm=TOKEN_TILE)
    zx = _matmul(hn, ssd_w_in, j, tn=1024, tm=TOKEN_TILE, n_out=D_INNER + SSD_CONV_DIM)

    pad = 128 - SSD_HEADS
    w_dt = jnp.pad(ssd_w_in[j][:, D_INNER + SSD_CONV_DIM:], ((0, 0), (0, pad))).astype(BF16)
    dt_all, dec_all, cs_all = _dt_prep(hn, w_dt,
                                      jnp.pad(dt_bias[j], (0, pad)).reshape(1, 128),
                                      jnp.pad(a_log[j], (0, pad)).reshape(1, 128))

    cw, cbias = conv_w[j], conv_b[j].reshape(1, SSD_CONV_DIM)
    xbc_c, chist_p = _ssdconv_prompt(zx, cw, cbias)
    ch = conv_hist[j]
    xbc_c_s, raw_s = _ssdconv_sample(zx, ch[:, 0, :], ch[:, 1, :], ch[:, 2, :], cw, cbias)
    chist_s = jnp.stack([ch[:, 1, :], ch[:, 2, :], raw_s], axis=1)

    dskip = jnp.repeat(d_skip[j], SSD_HEAD_DIM).reshape(SSD_GROUPS, 1, GROUP_W)
    ng = norm_ssd[j].reshape(SSD_GROUPS, 1, GROUP_W)

    def per_group(v):
        v = v[:T_P, :SSD_HEADS].reshape(T_P, SSD_GROUPS, SSD_HPG)
        return jnp.transpose(v, (1, 0, 2)), jnp.transpose(v, (1, 2, 0))

    dtc, dtr = per_group(dt_all)
    csc, csr = per_group(cs_all)
    g_p, sfin = _ssd_scan_prompt(xbc_c, zx, dtc, csc, dtr, csr, dskip, ng)
    ss_p = sfin.reshape(BATCH, SSD_HEADS, SSD_HEAD_DIM, SSD_STATE)

    def tile_major(v):
        return jnp.transpose(v.reshape(DEC_BATCH, D_INNER // 128, 128), (0, 2, 1))

    dec_s = jnp.broadcast_to(dec_all[T_P:, :SSD_HEADS, None], (DEC_BATCH, SSD_HEADS, SSD_STATE))
    x_t = tile_major(xbc_c_s[:, :D_INNER])
    dt_t = tile_major(jnp.repeat(dt_all[T_P:, :SSD_HEADS], SSD_HEAD_DIM, axis=1))
    b_s = xbc_c_s[:, D_INNER:D_INNER + SSD_GROUPS * SSD_STATE].reshape(DEC_BATCH, 1, -1)
    c_s = xbc_c_s[:, D_INNER + SSD_GROUPS * SSD_STATE:].reshape(DEC_BATCH, 1, -1)
    s_new, y_t = _ssd_step_sample(state[j].reshape(DEC_BATCH, D_INNER, SSD_STATE), x_t, dt_t, b_s, c_s, dec_s)
    y_s = jnp.transpose(y_t, (0, 2, 1)).reshape(DEC_BATCH, D_INNER)
    g_s = _gate_norm_sample(y_s, xbc_c_s, zx, dskip, ng)
    ss_s = s_new.reshape(DEC_BATCH, SSD_HEADS, SSD_HEAD_DIM, SSD_STATE)

    g_all = jnp.concatenate([g_p, g_s], axis=0)
    h = _matmul(g_all, w_out, j, tn=512, tm=TOKEN_TILE, res=h)
    return h, chist_p, chist_s, ss_p, ss_s


def kernel(x_prompt, x_sample, mem_prompt, cache_sc, state_ssd_conv, state_ssd, cache_mem_k, cache_mem_v, norm_mix, norm_mem_q, norm_mem_kv, norm_ffn, norm_final, sc_w_in, sc_w_conv, sc_w_out, ssd_w_in, ssd_conv_w, ssd_conv_b, ssd_dt_bias, ssd_a_log, ssd_d, ssd_norm, ssd_w_out, xa_w_q, xa_w_k, xa_w_v, xa_w_o, ffn_w_gate, ffn_w_up, ffn_w_down):
    depth = norm_mix.shape[0]
    h = jnp.concatenate([x_prompt.reshape(T_P, D_MODEL), x_sample.reshape(T_S, D_MODEL)], axis=0)

    mem = mem_prompt.reshape(BATCH * MEM_LEN, D_MODEL)
    k_ps, v_ps = [], []
    for i in range(depth):
        m = _rmsnorm(mem, norm_mem_kv[i], BF16, tm=512)
        k_ps.append(_matmul(m, xa_w_k, i, tn=1024, tm=512))
        v_ps.append(_matmul(m, xa_w_v, i, tn=1024, tm=512))

    kc = cache_mem_k.reshape(depth, DEC_BATCH, MEM_LEN, D_MODEL)
    vc = cache_mem_v.reshape(depth, DEC_BATCH, MEM_LEN, D_MODEL)

    sc_p, sc_s, sconv_p, sconv_s, ss_p, ss_s = [], [], [], [], [], []
    for i in range(depth):
        if i % 2 == 0:
            h, hp, hs = _short_conv_block(h, i // 2, norm_mix[i], cache_sc, sc_w_in, sc_w_conv, sc_w_out)
            sc_p.append(hp)
            sc_s.append(hs)
        else:
            j = i // 2
            h, cp, cs, sp, ss = _ssd_block(h, j, norm_mix[i], state_ssd_conv, state_ssd, ssd_w_in, ssd_conv_w,
                                           ssd_conv_b, ssd_dt_bias, ssd_a_log, ssd_d, ssd_norm, ssd_w_out)
            sconv_p.append(cp)
            sconv_s.append(cs)
            ss_p.append(sp)
            ss_s.append(ss)
        h = _mem_attn_block(h, i, norm_mem_q, xa_w_q, xa_w_o, k_ps[i], v_ps[i], kc, vc)
        h = _ffn_block(h, i, norm_ffn, ffn_w_gate, ffn_w_up, ffn_w_down)

    y_p = _rmsnorm(h, norm_final, F32, tm=512, rows=T_P).reshape(BATCH, SEQ, D_MODEL)
    y_s = _rmsnorm(h, norm_final, F32, tm=T_S, row_block0=T_P // T_S, rows=T_S).reshape(DEC_BATCH, 1, D_MODEL)

    kv_shape = (depth, BATCH, MEM_LEN, MEM_HEADS, MEM_HEAD_DIM)
    return (y_p, y_s, jnp.stack(sc_p), jnp.stack(sc_s), jnp.stack(sconv_p), jnp.stack(sconv_s),
            jnp.stack(ss_p), jnp.stack(ss_s),
            jnp.stack(k_ps).reshape(kv_shape), jnp.stack(v_ps).reshape(kv_shape))
```

```python
import functools

import jax
import jax.numpy as jnp
from jax import lax
from jax.experimental import pallas as pl
from jax.experimental.pallas import tpu as pltpu

F32 = jnp.float32
BF16 = jnp.bfloat16

D_MODEL = 2048
BATCH = 4
SEQ = 2048
DEC_BATCH = 128
T_P = BATCH * SEQ
T_S = DEC_BATCH
D_INNER = 4096
SSD_HEAD_DIM = 64
SSD_HEADS = 64
SSD_GROUPS = 8
SSD_HPG = 8
SSD_STATE = 128
GROUP_W = SSD_HPG * SSD_HEAD_DIM
BC_W = SSD_GROUPS * SSD_STATE
SSD_CONV_DIM = D_INNER + 2 * BC_W
MEM_LEN = 256
MEM_HEADS = 4
MEM_HEAD_DIM = 512
RMS_EPS = 1e-5
LANES = 128

V7X_VMEM_LIMIT_BYTES = 56 * 1024 * 1024
SSD_CHUNK = 128
N_TILES = D_INNER // LANES


def _cparams(*sem):
    return pltpu.CompilerParams(dimension_semantics=sem, vmem_limit_bytes=V7X_VMEM_LIMIT_BYTES)


def _silu(x):
    return x * jax.nn.sigmoid(x)


def _rms(x, g):
    ms = jnp.mean(x * x, axis=-1, keepdims=True)
    return x * lax.rsqrt(ms + RMS_EPS) * g


def _rmsnorm_kernel(xp_ref, xs_ref, g_ref, op_ref, os_ref, *, n_p):
    i = pl.program_id(0)

    @pl.when(i < n_p)
    def _():
        op_ref[...] = _rms(xp_ref[...], g_ref[...]).astype(op_ref.dtype)

    @pl.when(i == n_p)
    def _():
        os_ref[...] = _rms(xs_ref[...], g_ref[...]).astype(os_ref.dtype)


def _rmsnorm2(xp, xs, g, out_dtype, *, tm=1024):
    d = xp.shape[1]
    n_p = xp.shape[0] // tm
    pspec = pl.BlockSpec((tm, d), lambda i: (jnp.minimum(i, n_p - 1), 0))
    sspec = pl.BlockSpec((T_S, d), lambda i: (0, 0))
    return pl.pallas_call(
        functools.partial(_rmsnorm_kernel, n_p=n_p),
        grid=(n_p + 1,),
        in_specs=[pspec, sspec, pl.BlockSpec((1, d), lambda i: (0, 0))],
        out_specs=[pspec, sspec],
        out_shape=[jax.ShapeDtypeStruct(xp.shape, out_dtype), jax.ShapeDtypeStruct(xs.shape, out_dtype)],
        compiler_params=_cparams("arbitrary"),
        name="rmsnorm",
    )(xp, xs, g.reshape(1, d))


def _rmsnorm1_kernel(x_ref, g_ref, o_ref):
    o_ref[...] = _rms(x_ref[...], g_ref[...]).astype(o_ref.dtype)


def _rmsnorm1(x, g, out_dtype, *, tm):
    rows, d = x.shape
    return pl.pallas_call(
        _rmsnorm1_kernel,
        grid=(rows // tm,),
        in_specs=[pl.BlockSpec((tm, d), lambda i: (i, 0)), pl.BlockSpec((1, d), lambda i: (0, 0))],
        out_specs=pl.BlockSpec((tm, d), lambda i: (i, 0)),
        out_shape=jax.ShapeDtypeStruct((rows, d), out_dtype),
        compiler_params=_cparams("arbitrary"),
        name="rmsnorm_mem",
    )(x, g.reshape(1, d))


def _mm_kernel(*refs, n_p, has_s, has_res, n_w, epilogue):
    it = iter(refs)
    xp_ref = next(it)
    xs_ref = next(it) if has_s else None
    w_refs = [next(it) for _ in range(n_w)]
    rp_ref = next(it) if has_res else None
    rs_ref = next(it) if (has_res and has_s) else None
    op_ref = next(it)
    os_ref = next(it) if has_s else None
    wb_refs = [next(it) for _ in range(n_w)]
    i = pl.program_id(1)

    @pl.when(i == 0)
    def _():
        for w_ref, wb_ref in zip(w_refs, wb_refs):
            wb_ref[...] = w_ref[...].astype(BF16)

    def run(x_ref, r_ref, o_ref):
        x = x_ref[...]
        accs = [jnp.dot(x, wb_ref[...], preferred_element_type=F32) for wb_ref in wb_refs]
        out = epilogue(*accs)
        if r_ref is not None:
            out = out + r_ref[...]
        o_ref[...] = out.astype(o_ref.dtype)

    @pl.when(i < n_p)
    def _():
        run(xp_ref, rp_ref, op_ref)

    if has_s:
        @pl.when(i == n_p)
        def _():
            run(xs_ref, rs_ref, os_ref)


def _identity(a):
    return a


def _swiglu_epilogue(a, u):
    return _silu(a) * u


def _matmul2(xp, xs, ws, layer, *, tn, tm, n_out=None, res=None, out_dtype=F32, epilogue=_identity):
    m, k = xp.shape
    n = ws[0].shape[2] if n_out is None else n_out
    n_p = m // tm
    has_s = xs is not None
    steps = n_p + (1 if has_s else 0)

    def prow(j, i):
        return jnp.minimum(i, n_p - 1)

    xp_spec = pl.BlockSpec((tm, k), lambda j, i: (prow(j, i), 0))
    xs_spec = pl.BlockSpec((T_S, k), lambda j, i: (0, 0))
    w_spec = pl.BlockSpec((None, k, tn), lambda j, i: (layer, 0, j))
    op_spec = pl.BlockSpec((tm, tn), lambda j, i: (prow(j, i), j))
    os_spec = pl.BlockSpec((T_S, tn), lambda j, i: (0, j))

    in_specs, args = [xp_spec], [xp]
    if has_s:
        in_specs.append(xs_spec)
        args.append(xs)
    in_specs += [w_spec] * len(ws)
    args += list(ws)
    if res is not None:
        in_specs.append(op_spec)
        args.append(res[0])
        if has_s:
            in_specs.append(os_spec)
            args.append(res[1])
    out_specs = [op_spec] + ([os_spec] if has_s else [])
    out_shape = [jax.ShapeDtypeStruct((m, n), out_dtype)] + ([jax.ShapeDtypeStruct((T_S, n), out_dtype)] if has_s else [])
    outs = pl.pallas_call(
        functools.partial(_mm_kernel, n_p=n_p, has_s=has_s, has_res=res is not None, n_w=len(ws), epilogue=epilogue),
        grid=(n // tn, steps),
        in_specs=in_specs,
        out_specs=out_specs,
        out_shape=out_shape,
        scratch_shapes=[pltpu.VMEM((k, tn), BF16) for _ in ws],
        compiler_params=_cparams("arbitrary", "arbitrary"),
        name="matmul",
    )(*args)
    return tuple(outs) if has_s else outs[0]


def _shift_rows(u, k, row):
    return jnp.where(row >= k, pltpu.roll(u, k, axis=0), 0.0)


def _sc_prompt_kernel(bg_ref, cg_ref, v_ref, w_ref, y_ref, hist_ref):
    u = cg_ref[...] * v_ref[...]
    row = lax.broadcasted_iota(jnp.int32, u.shape, 0)
    w = w_ref[...]
    conv = w[2:3, :] * u + w[1:2, :] * _shift_rows(u, 1, row) + w[0:1, :] * _shift_rows(u, 2, row)
    y_ref[...] = (bg_ref[...] * conv).astype(y_ref.dtype)
    hist_ref[...] = u[SEQ - 2:, :]


def _sc_prompt(bcv, w_conv, *, tc=256):
    nb = D_MODEL // tc
    return pl.pallas_call(
        _sc_prompt_kernel,
        grid=(BATCH, nb),
        in_specs=[pl.BlockSpec((SEQ, tc), lambda b, j: (b, j)),
                  pl.BlockSpec((SEQ, tc), lambda b, j: (b, j + nb)),
                  pl.BlockSpec((SEQ, tc), lambda b, j: (b, j + 2 * nb)),
                  pl.BlockSpec((3, tc), lambda b, j: (0, j))],
        out_specs=[pl.BlockSpec((SEQ, tc), lambda b, j: (b, j)),
                   pl.BlockSpec((None, 2, tc), lambda b, j: (b, 0, j))],
        out_shape=[jax.ShapeDtypeStruct((T_P, D_MODEL), BF16),
                   jax.ShapeDtypeStruct((BATCH, 2, D_MODEL), F32)],
        compiler_params=_cparams("arbitrary", "arbitrary"),
        name="sc_prompt",
    )(bcv, bcv, bcv, w_conv)


def _sc_sample_kernel(bg_ref, cg_ref, v_ref, h0_ref, h1_ref, w_ref, y_ref, u_ref):
    u = cg_ref[...] * v_ref[...]
    w = w_ref[...]
    conv = w[2:3, :] * u + w[1:2, :] * h1_ref[...] + w[0:1, :] * h0_ref[...]
    y_ref[...] = (bg_ref[...] * conv).astype(y_ref.dtype)
    u_ref[...] = u


def _sc_sample(bcv, h0, h1, w_conv, *, tc=512):
    nb = D_MODEL // tc
    hspec = pl.BlockSpec((T_S, tc), lambda j: (0, j))
    return pl.pallas_call(
        _sc_sample_kernel,
        grid=(nb,),
        in_specs=[hspec,
                  pl.BlockSpec((T_S, tc), lambda j: (0, j + nb)),
                  pl.BlockSpec((T_S, tc), lambda j: (0, j + 2 * nb)),
                  hspec, hspec,
                  pl.BlockSpec((3, tc), lambda j: (0, j))],
        out_specs=[hspec, hspec],
        out_shape=[jax.ShapeDtypeStruct((T_S, D_MODEL), BF16),
                   jax.ShapeDtypeStruct((T_S, D_MODEL), F32)],
        compiler_params=_cparams("arbitrary"),
        name="sc_sample",
    )(bcv, bcv, bcv, h0, h1, w_conv)


def _ssdconv_prompt_kernel(x_ref, w_ref, b_ref, o_ref, hist_ref):
    x = x_ref[...]
    row = lax.broadcasted_iota(jnp.int32, x.shape, 0)
    w = w_ref[...]
    conv = (w[3:4, :] * x + w[2:3, :] * _shift_rows(x, 1, row)
            + w[1:2, :] * _shift_rows(x, 2, row) + w[0:1, :] * _shift_rows(x, 3, row))
    o_ref[...] = _silu(conv + b_ref[...])
    hist_ref[...] = x[SEQ - 3:, :]


def _ssdconv_prompt(zx, conv_w, conv_b, *, tc=256):
    nb = SSD_CONV_DIM // tc
    off = D_INNER // tc
    return pl.pallas_call(
        _ssdconv_prompt_kernel,
        grid=(BATCH, nb),
        in_specs=[pl.BlockSpec((SEQ, tc), lambda b, j: (b, j + off)),
                  pl.BlockSpec((4, tc), lambda b, j: (0, j)),
                  pl.BlockSpec((1, tc), lambda b, j: (0, j))],
        out_specs=[pl.BlockSpec((SEQ, tc), lambda b, j: (b, j)),
                   pl.BlockSpec((None, 3, tc), lambda b, j: (b, 0, j))],
        out_shape=[jax.ShapeDtypeStruct((T_P, SSD_CONV_DIM), F32),
                   jax.ShapeDtypeStruct((BATCH, 3, SSD_CONV_DIM), F32)],
        compiler_params=_cparams("arbitrary", "arbitrary"),
        name="ssdconv_prompt",
    )(zx, conv_w, conv_b)


def _ssdconv_sample_kernel(x_ref, h0_ref, h1_ref, h2_ref, w_ref, b_ref, o_ref, raw_ref):
    x = x_ref[...]
    w = w_ref[...]
    conv = w[3:4, :] * x + w[2:3, :] * h2_ref[...] + w[1:2, :] * h1_ref[...] + w[0:1, :] * h0_ref[...]
    o_ref[...] = _silu(conv + b_ref[...])
    raw_ref[...] = x


def _ssdconv_sample(zx, h0, h1, h2, conv_w, conv_b, *, tc=512):
    nb = SSD_CONV_DIM // tc
    off = D_INNER // tc
    hspec = pl.BlockSpec((T_S, tc), lambda j: (0, j))
    return pl.pallas_call(
        _ssdconv_sample_kernel,
        grid=(nb,),
        in_specs=[pl.BlockSpec((T_S, tc), lambda j: (0, j + off)), hspec, hspec, hspec,
                  pl.BlockSpec((4, tc), lambda j: (0, j)),
                  pl.BlockSpec((1, tc), lambda j: (0, j))],
        out_specs=[hspec, hspec],
        out_shape=[jax.ShapeDtypeStruct((T_S, SSD_CONV_DIM), F32),
                   jax.ShapeDtypeStruct((T_S, SSD_CONV_DIM), F32)],
        compiler_params=_cparams("arbitrary"),
        name="ssdconv_sample",
    )(zx, h0, h1, h2, conv_w, conv_b)


def _dt_kernel(xp_ref, xs_ref, w_ref, bias_ref, alog_ref, dtp_ref, csp_ref, dts_ref, decs_ref, *, n_p):
    i = pl.program_id(0)

    def dt_da(x_ref):
        raw = jnp.dot(x_ref[...], w_ref[...], preferred_element_type=F32) + bias_ref[...]
        dt = jnp.maximum(raw, 0.0) + jnp.log1p(jnp.exp(-jnp.abs(raw)))
        return dt, dt * (-jnp.exp(alog_ref[...]))

    @pl.when(i < n_p)
    def _():
        dt, da = dt_da(xp_ref)
        dtp_ref[...] = dt
        row = lax.broadcasted_iota(jnp.int32, da.shape, 0)
        cs = da
        k = 1
        while k < SSD_CHUNK:
            cs = cs + jnp.where(row >= k, pltpu.roll(cs, k, axis=0), 0.0)
            k *= 2
        csp_ref[...] = cs

    @pl.when(i == n_p)
    def _():
        dt, da = dt_da(xs_ref)
        dts_ref[...] = dt
        decs_ref[...] = jnp.exp(da)


def _dt_prep(hn_p, hn_s, w_dt, dt_bias, a_log):
    k = hn_p.shape[1]
    n_p = T_P // SSD_CHUNK
    pin = pl.BlockSpec((SSD_CHUNK, k), lambda i: (jnp.minimum(i, n_p - 1), 0))
    pout = pl.BlockSpec((SSD_CHUNK, LANES), lambda i: (jnp.minimum(i, n_p - 1), 0))
    sout = pl.BlockSpec((T_S, LANES), lambda i: (0, 0))
    one = pl.BlockSpec((1, LANES), lambda i: (0, 0))
    return pl.pallas_call(
        functools.partial(_dt_kernel, n_p=n_p),
        grid=(n_p + 1,),
        in_specs=[pin, pl.BlockSpec((T_S, k), lambda i: (0, 0)),
                  pl.BlockSpec((k, LANES), lambda i: (0, 0)), one, one],
        out_specs=[pout, pout, sout, sout],
        out_shape=[jax.ShapeDtypeStruct((T_P, LANES), F32), jax.ShapeDtypeStruct((T_P, LANES), F32),
                   jax.ShapeDtypeStruct((T_S, LANES), F32), jax.ShapeDtypeStruct((T_S, LANES), F32)],
        compiler_params=_cparams("arbitrary"),
        name="ssd_dt",
    )(hn_p, hn_s, w_dt, dt_bias, a_log)


def _expand_heads(v, onehot):
    hi = v.astype(BF16)
    lo = (v - hi.astype(F32)).astype(BF16)
    return (jnp.dot(hi, onehot, preferred_element_type=F32)
            + jnp.dot(lo, onehot, preferred_element_type=F32))


def _gate_norm(y, z, ng):
    return _rms(y * _silu(z), ng)


def _ssd_scan_kernel(x_ref, b_ref, c_ref, z_ref, dtc_ref, csc_ref, dtr_ref, csr_ref, dsk_ref, ng_ref,
                     y_ref, sfin_ref, st_ref):
    q = SSD_CHUNK
    c = pl.program_id(2)

    @pl.when(c == 0)
    def _():
        st_ref[...] = jnp.zeros_like(st_ref)

    x = x_ref[...]
    bm = b_ref[...]
    xb = x.astype(BF16)
    bb = bm.astype(BF16)
    cb = c_ref[...].astype(BF16)
    csc = csc_ref[...]
    dtc = dtc_ref[...]
    csr = csr_ref[...]
    dtr = dtr_ref[...]
    st = st_ref[...]

    lane = lax.broadcasted_iota(jnp.int32, (SSD_HPG, GROUP_W), 1)
    head = lax.broadcasted_iota(jnp.int32, (SSD_HPG, GROUP_W), 0)
    onehot = jnp.where(lane // SSD_HEAD_DIM == head, 1.0, 0.0).astype(BF16)

    cbm = lax.dot_general(cb, bb, (((1,), (1,)), ((), ())), preferred_element_type=F32)
    row = lax.broadcasted_iota(jnp.int32, (q, q), 0)
    col = lax.broadcasted_iota(jnp.int32, (q, q), 1)
    tri = row >= col
    lane128 = lax.broadcasted_iota(jnp.int32, (q, LANES), 1)
    lo_half = lane128 < SSD_HEAD_DIM

    tiles = []
    for m in range(GROUP_W // LANES):
        xt = xb[:, LANES * m:LANES * (m + 1)]
        acc = None
        for j, xm in ((2 * m, jnp.where(lo_half, xt, 0)), (2 * m + 1, jnp.where(lo_half, 0, xt))):
            seg = csc[:, j:j + 1] - csr[j:j + 1, :]
            dec = jnp.exp(jnp.where(tri, seg, -jnp.inf))
            w = (cbm * dec * dtr[j:j + 1, :]).astype(BF16)
            part = jnp.dot(w, xm, preferred_element_type=F32)
            acc = part if acc is None else acc + part
        tiles.append(acc)
    y_diag = jnp.concatenate(tiles, axis=1)

    e_exp = _expand_heads(jnp.exp(csc), onehot)
    y_off = jnp.dot(cb, st.astype(BF16), preferred_element_type=F32) * e_exp

    cs_last = csc[q - 1:q, :]
    te = _expand_heads(jnp.exp(cs_last - csc) * dtc, onehot)
    xw = (x * te).astype(BF16)
    d_st = jnp.dot(bm.T.astype(BF16), xw, preferred_element_type=F32)
    st_new = st * e_exp[q - 1:q, :] + d_st
    st_ref[...] = st_new

    y = y_diag + y_off + dsk_ref[...] * x
    y_ref[...] = _gate_norm(y, z_ref[...], ng_ref[...]).astype(y_ref.dtype)

    @pl.when(c == pl.num_programs(2) - 1)
    def _():
        sfin_ref[...] = st_new.T


def _ssd_scan_prompt(xbc_c, zx, dtc, csc, dtr, csr, dskip, ng):
    q = SSD_CHUNK
    nc = SEQ // q
    xoff = D_INNER // SSD_STATE
    small_c = pl.BlockSpec((None, q, SSD_HPG), lambda b, g, c: (g, b * nc + c, 0))
    small_r = pl.BlockSpec((None, SSD_HPG, q), lambda b, g, c: (g, 0, b * nc + c))
    pspec = pl.BlockSpec((None, 1, GROUP_W), lambda b, g, c: (g, 0, 0))
    return pl.pallas_call(
        _ssd_scan_kernel,
        grid=(BATCH, SSD_GROUPS, nc),
        in_specs=[pl.BlockSpec((q, GROUP_W), lambda b, g, c: (b * nc + c, g)),
                  pl.BlockSpec((q, SSD_STATE), lambda b, g, c: (b * nc + c, xoff + g)),
                  pl.BlockSpec((q, SSD_STATE), lambda b, g, c: (b * nc + c, xoff + SSD_GROUPS + g)),
                  pl.BlockSpec((q, GROUP_W), lambda b, g, c: (b * nc + c, g)),
                  small_c, small_c, small_r, small_r, pspec, pspec],
        out_specs=[pl.BlockSpec((q, GROUP_W), lambda b, g, c: (b * nc + c, g)),
                   pl.BlockSpec((GROUP_W, SSD_STATE), lambda b, g, c: (b * SSD_GROUPS + g, 0))],
        out_shape=[jax.ShapeDtypeStruct((T_P, D_INNER), BF16),
                   jax.ShapeDtypeStruct((BATCH * D_INNER, SSD_STATE), F32)],
        scratch_shapes=[pltpu.VMEM((SSD_STATE, GROUP_W), F32)],
        compiler_params=_cparams("arbitrary", "arbitrary", "arbitrary"),
        name="ssd_scan",
    )(xbc_c, xbc_c, xbc_c, zx, dtc, csc, dtr, csr, dskip, ng)


def _ssd_step_kernel(s_ref, xt_ref, dtt_ref, bt_ref, ce_ref, dec_ref, so_ref, yt_ref):
    xdt = (xt_ref[...] * dtt_ref[...]).astype(BF16)
    tile_of_lane = lax.broadcasted_iota(jnp.int32, (N_TILES, D_INNER), 1) // LANES
    row32 = lax.broadcasted_iota(jnp.int32, (N_TILES, D_INNER), 0)
    b_big = jnp.where(tile_of_lane == row32, bt_ref[...], 0.0).astype(BF16)
    upd = jnp.dot(xdt, b_big, preferred_element_type=F32)

    lane32 = lax.broadcasted_iota(jnp.int32, (SSD_STATE, N_TILES), 1)
    ce = ce_ref[...]
    s_tiles, c_tiles = [], []
    for r in range(N_TILES):
        dtile = jnp.concatenate(
            [jnp.broadcast_to(dec_ref[2 * r:2 * r + 1, :], (SSD_HEAD_DIM, SSD_STATE)),
             jnp.broadcast_to(dec_ref[2 * r + 1:2 * r + 2, :], (SSD_HEAD_DIM, SSD_STATE))], axis=0)
        s_n = s_ref[LANES * r:LANES * (r + 1), :] * dtile + upd[:, LANES * r:LANES * (r + 1)]
        so_ref[LANES * r:LANES * (r + 1), :] = s_n
        s_tiles.append(s_n.astype(BF16))
        c_tiles.append(jnp.where(lane32 == r, ce, 0.0).astype(BF16))
    s_big = jnp.concatenate(s_tiles, axis=1)
    c_big = jnp.concatenate(c_tiles, axis=0)
    yt_ref[...] = jnp.dot(s_big, c_big, preferred_element_type=F32)


def _ssd_step_sample(state, x_t, dt_t, b_tiled, c_exp, dec):
    tspec = pl.BlockSpec((None, LANES, N_TILES), lambda b: (b, 0, 0))
    sspec = pl.BlockSpec((None, D_INNER, SSD_STATE), lambda b: (b, 0, 0))
    return pl.pallas_call(
        _ssd_step_kernel,
        grid=(DEC_BATCH,),
        in_specs=[sspec, tspec, tspec,
                  pl.BlockSpec((None, 1, D_INNER), lambda b: (b, 0, 0)),
                  pl.BlockSpec((None, SSD_STATE, N_TILES), lambda b: (b, 0, 0)),
                  pl.BlockSpec((None, SSD_HEADS, SSD_STATE), lambda b: (b, 0, 0))],
        out_specs=[sspec, tspec],
        out_shape=[jax.ShapeDtypeStruct((DEC_BATCH, D_INNER, SSD_STATE), F32),
                   jax.ShapeDtypeStruct((DEC_BATCH, LANES, N_TILES), F32)],
        compiler_params=_cparams("arbitrary"),
        name="ssd_step",
    )(state, x_t, dt_t, b_tiled, c_exp, dec)


def _gate_norm_kernel(y_ref, x_ref, z_ref, dsk_ref, ng_ref, o_ref):
    y = y_ref[...] + dsk_ref[...] * x_ref[...]
    o_ref[...] = _gate_norm(y, z_ref[...], ng_ref[...]).astype(o_ref.dtype)


def _gate_norm_sample(y_s, xbc_c_s, zx_s, dskip, ng):
    gspec = pl.BlockSpec((T_S, GROUP_W), lambda g: (0, g))
    pspec = pl.BlockSpec((None, 1, GROUP_W), lambda g: (g, 0, 0))
    return pl.pallas_call(
        _gate_norm_kernel,
        grid=(SSD_GROUPS,),
        in_specs=[gspec, gspec, gspec, pspec, pspec],
        out_specs=gspec,
        out_shape=jax.ShapeDtypeStruct((T_S, D_INNER), BF16),
        compiler_params=_cparams("arbitrary"),
        name="ssd_gate_norm",
    )(y_s, xbc_c_s, zx_s, dskip, ng)


def _xattn_prompt_kernel(q_ref, k_ref, v_ref, o_ref):
    scale = MEM_HEAD_DIM ** -0.5
    for h in range(MEM_HEADS):
        sl = slice(MEM_HEAD_DIM * h, MEM_HEAD_DIM * (h + 1))
        kh = k_ref[:, sl].astype(BF16)
        vh = v_ref[:, sl].astype(BF16)
        s = lax.dot_general(q_ref[:, sl], kh, (((1,), (1,)), ((), ())), preferred_element_type=F32) * scale
        m = jnp.max(s, axis=-1, keepdims=True)
        e = jnp.exp(s - m)
        p = (e / jnp.sum(e, axis=-1, keepdims=True)).astype(BF16)
        o_ref[:, sl] = jnp.dot(p, vh, preferred_element_type=F32).astype(o_ref.dtype)


def _xattn_prompt(q, k, v, *, tq=512):
    nq = SEQ // tq
    kspec = pl.BlockSpec((MEM_LEN, D_MODEL), lambda b, i: (b, 0))
    return pl.pallas_call(
        _xattn_prompt_kernel,
        grid=(BATCH, nq),
        in_specs=[pl.BlockSpec((tq, D_MODEL), lambda b, i: (b * nq + i, 0)), kspec, kspec],
        out_specs=pl.BlockSpec((tq, D_MODEL), lambda b, i: (b * nq + i, 0)),
        out_shape=jax.ShapeDtypeStruct((T_P, D_MODEL), BF16),
        compiler_params=_cparams("arbitrary", "arbitrary"),
        name="xattn_prompt",
    )(q, k, v)


def _xattn_sample_kernel(q_ref, k_ref, v_ref, o_ref, *, bb):
    scale = MEM_HEAD_DIM ** -0.5
    for b in range(bb):
        s = jnp.sum(k_ref[b] * q_ref[b], axis=-1, keepdims=True) * scale
        m = jnp.max(s, axis=0, keepdims=True)
        e = jnp.exp(s - m)
        p = e / jnp.sum(e, axis=0, keepdims=True)
        o_ref[b] = jnp.sum(p * v_ref[b], axis=0)


def _xattn_sample(q_s, kc, vc, layer, *, bb=2):
    cspec = pl.BlockSpec((None, bb, MEM_LEN, MEM_HEADS, MEM_HEAD_DIM), lambda i: (layer, i, 0, 0, 0))
    qspec = pl.BlockSpec((bb, MEM_HEADS, MEM_HEAD_DIM), lambda i: (i, 0, 0))
    return pl.pallas_call(
        functools.partial(_xattn_sample_kernel, bb=bb),
        grid=(DEC_BATCH // bb,),
        in_specs=[qspec, cspec, cspec],
        out_specs=qspec,
        out_shape=jax.ShapeDtypeStruct((DEC_BATCH, MEM_HEADS, MEM_HEAD_DIM), F32),
        compiler_params=_cparams("arbitrary"),
        name="xattn_sample",
    )(q_s, kc, vc)


def _mem_attn_block(hp, hs, i, norm_mem_q, xa_w_q, xa_w_o, k_p, v_p, kc, vc):
    hq_p, hq_s = _rmsnorm2(hp, hs, norm_mem_q[i], BF16)
    q_p, q_s = _matmul2(hq_p, hq_s, [xa_w_q], i, tn=1024, tm=1024, out_dtype=BF16)
    o_p = _xattn_prompt(q_p, k_p, v_p)
    o_s = _xattn_sample(q_s.astype(F32).reshape(DEC_BATCH, MEM_HEADS, MEM_HEAD_DIM), kc, vc, i)
    o_s = o_s.reshape(DEC_BATCH, D_MODEL).astype(BF16)
    return _matmul2(o_p, o_s, [xa_w_o], i, tn=1024, tm=1024, res=(hp, hs))


def _ffn_block(hp, hs, i, norm_ffn, w_gate, w_up, w_down):
    hf_p, hf_s = _rmsnorm2(hp, hs, norm_ffn[i], BF16)
    a_p, a_s = _matmul2(hf_p, hf_s, [w_gate, w_up], i, tn=512, tm=1024, out_dtype=BF16, epilogue=_swiglu_epilogue)
    return _matmul2(a_p, a_s, [w_down], i, tn=512, tm=512, res=(hp, hs))


def _short_conv_block(hp, hs, a, norm_g, cache_sc, sc_w_in, sc_w_conv, sc_w_out):
    hn_p, hn_s = _rmsnorm2(hp, hs, norm_g, BF16)
    bcv_p, bcv_s = _matmul2(hn_p, hn_s, [sc_w_in], a, tn=1024, tm=1024)
    w_conv = sc_w_conv[a]
    y_p, hist_p = _sc_prompt(bcv_p, w_conv)
    h0, h1 = cache_sc[a, :, 0, :], cache_sc[a, :, 1, :]
    y_s, u_s = _sc_sample(bcv_s, h0, h1, w_conv)
    hp, hs = _matmul2(y_p, y_s, [sc_w_out], a, tn=1024, tm=1024, res=(hp, hs))
    return hp, hs, hist_p, jnp.stack([h1, u_s], axis=1)


def _ssd_block(hp, hs, j, norm_g, conv_hist, state, ssd_w_in, conv_w, conv_b, dt_bias, a_log, d_skip, norm_ssd, w_out):
    hn_p, hn_s = _rmsnorm2(hp, hs, norm_g, BF16)
    zx_p, zx_s = _matmul2(hn_p, hn_s, [ssd_w_in], j, tn=1024, tm=1024, n_out=D_INNER + SSD_CONV_DIM)

    pad = LANES - SSD_HEADS
    w_dt = jnp.pad(ssd_w_in[j][:, D_INNER + SSD_CONV_DIM:], ((0, 0), (0, pad))).astype(BF16)
    dt_p, cs_p, dt_s, dec_s = _dt_prep(hn_p, hn_s, w_dt,
                                       jnp.pad(dt_bias[j], (0, pad)).reshape(1, LANES),
                                       jnp.pad(a_log[j], (0, pad)).reshape(1, LANES))

    cw, cbias = conv_w[j], conv_b[j].reshape(1, SSD_CONV_DIM)
    xbc_c, chist_p = _ssdconv_prompt(zx_p, cw, cbias)
    ch = conv_hist[j]
    xbc_c_s, raw_s = _ssdconv_sample(zx_s, ch[:, 0, :], ch[:, 1, :], ch[:, 2, :], cw, cbias)
    chist_s = jnp.stack([ch[:, 1, :], ch[:, 2, :], raw_s], axis=1)

    dskip = jnp.repeat(d_skip[j], SSD_HEAD_DIM).reshape(SSD_GROUPS, 1, GROUP_W)
    ng = norm_ssd[j].reshape(SSD_GROUPS, 1, GROUP_W)

    def per_group(v):
        v = v[:, :SSD_HEADS].reshape(T_P, SSD_GROUPS, SSD_HPG)
        return jnp.transpose(v, (1, 0, 2)), jnp.transpose(v, (1, 2, 0))

    dtc, dtr = per_group(dt_p)
    csc, csr = per_group(cs_p)
    g_p, sfin = _ssd_scan_prompt(xbc_c, zx_p, dtc, csc, dtr, csr, dskip, ng)
    ss_p = sfin.reshape(BATCH, SSD_HEADS, SSD_HEAD_DIM, SSD_STATE)

    def tile_major(v):
        return jnp.transpose(v.reshape(DEC_BATCH, N_TILES, LANES), (0, 2, 1))

    tiles_per_group = GROUP_W // LANES
    x_t = tile_major(xbc_c_s[:, :D_INNER])
    dt_t = tile_major(jnp.repeat(dt_s[:, :SSD_HEADS], SSD_HEAD_DIM, axis=1))
    b_g = xbc_c_s[:, D_INNER:D_INNER + BC_W].reshape(DEC_BATCH, SSD_GROUPS, SSD_STATE)
    c_g = xbc_c_s[:, D_INNER + BC_W:].reshape(DEC_BATCH, SSD_GROUPS, SSD_STATE)
    b_tiled = jnp.repeat(b_g, tiles_per_group, axis=1).reshape(DEC_BATCH, 1, D_INNER)
    c_exp = jnp.transpose(jnp.repeat(c_g, tiles_per_group, axis=1), (0, 2, 1))
    dec_b = jnp.broadcast_to(dec_s[:, :SSD_HEADS, None], (DEC_BATCH, SSD_HEADS, SSD_STATE))
    s_new, y_t = _ssd_step_sample(state[j].reshape(DEC_BATCH, D_INNER, SSD_STATE), x_t, dt_t, b_tiled, c_exp, dec_b)
    y_s = jnp.transpose(y_t, (0, 2, 1)).reshape(DEC_BATCH, D_INNER)
    g_s = _gate_norm_sample(y_s, xbc_c_s, zx_s, dskip, ng)
    ss_s = s_new.reshape(DEC_BATCH, SSD_HEADS, SSD_HEAD_DIM, SSD_STATE)

    hp, hs = _matmul2(g_p, g_s, [w_out], j, tn=512, tm=1024, res=(hp, hs))
    return hp, hs, chist_p, chist_s, ss_p, ss_s


def kernel(x_prompt, x_sample, mem_prompt, cache_sc, state_ssd_conv, state_ssd, cache_mem_k, cache_mem_v, norm_mix, norm_mem_q, norm_mem_kv, norm_ffn, norm_final, sc_w_in, sc_w_conv, sc_w_out, ssd_w_in, ssd_conv_w, ssd_conv_b, ssd_dt_bias, ssd_a_log, ssd_d, ssd_norm, ssd_w_out, xa_w_q, xa_w_k, xa_w_v, xa_w_o, ffn_w_gate, ffn_w_up, ffn_w_down):
    depth = norm_mix.shape[0]
    hp = x_prompt.reshape(T_P, D_MODEL)
    hs = x_sample.reshape(T_S, D_MODEL)

    mem = mem_prompt.reshape(BATCH * MEM_LEN, D_MODEL)
    k_ps, v_ps = [], []
    for i in range(depth):
        m = _rmsnorm1(mem, norm_mem_kv[i], BF16, tm=512)
        k_ps.append(_matmul2(m, None, [xa_w_k], i, tn=1024, tm=512))
        v_ps.append(_matmul2(m, None, [xa_w_v], i, tn=1024, tm=512))

    sc_p, sc_s, sconv_p, sconv_s, ss_p, ss_s = [], [], [], [], [], []
    for i in range(depth):
        if i % 2 == 0:
            hp, hs, a_p, a_s = _short_conv_block(hp, hs, i // 2, norm_mix[i], cache_sc, sc_w_in, sc_w_conv, sc_w_out)
            sc_p.append(a_p)
            sc_s.append(a_s)
        else:
            hp, hs, c_p, c_s, s_p, s_s = _ssd_block(hp, hs, i // 2, norm_mix[i], state_ssd_conv, state_ssd, ssd_w_in,
                                                    ssd_conv_w, ssd_conv_b, ssd_dt_bias, ssd_a_log, ssd_d, ssd_norm,
                                                    ssd_w_out)
            sconv_p.append(c_p)
            sconv_s.append(c_s)
            ss_p.append(s_p)
            ss_s.append(s_s)
        hp, hs = _mem_attn_block(hp, hs, i, norm_mem_q, xa_w_q, xa_w_o, k_ps[i], v_ps[i], cache_mem_k, cache_mem_v)
        hp, hs = _ffn_block(hp, hs, i, norm_ffn, ffn_w_gate, ffn_w_up, ffn_w_down)

    y_p, y_s = _rmsnorm2(hp, hs, norm_final, F32)

    kv_shape = (depth, BATCH, MEM_LEN, MEM_HEADS, MEM_HEAD_DIM)
    return (y_p.reshape(BATCH, SEQ, D_MODEL), y_s.reshape(DEC_BATCH, 1, D_MODEL),
            jnp.stack(sc_p), jnp.stack(sc_s), jnp.stack(sconv_p), jnp.stack(sconv_s),
            jnp.stack(ss_p), jnp.stack(ss_s),
            jnp.stack(k_ps).reshape(kv_shape), jnp.stack(v_ps).reshape(kv_shape))
```

```python
import functools

import jax
import jax.numpy as jnp
from jax import lax
from jax.experimental import pallas as pl
from jax.experimental.pallas import tpu as pltpu

F32 = jnp.float32
BF16 = jnp.bfloat16

D_MODEL = 2048
BATCH = 4
SEQ = 2048
DEC_BATCH = 128
T_P = BATCH * SEQ
T_S = DEC_BATCH
D_INNER = 4096
SSD_HEAD_DIM = 64
SSD_HEADS = 64
SSD_GROUPS = 8
SSD_HPG = 8
SSD_STATE = 128
GROUP_W = SSD_HPG * SSD_HEAD_DIM
BC_W = SSD_GROUPS * SSD_STATE
SSD_CONV_DIM = D_INNER + 2 * BC_W
MEM_LEN = 256
MEM_HEADS = 4
MEM_HEAD_DIM = 512
RMS_EPS = 1e-5
LANES = 128

V7X_VMEM_LIMIT_BYTES = 56 * 1024 * 1024
SSD_CHUNK = 128
N_TILES = D_INNER // LANES


def _cparams(*sem):
    return pltpu.CompilerParams(dimension_semantics=sem, vmem_limit_bytes=V7X_VMEM_LIMIT_BYTES)


def _silu(x):
    return x * jax.nn.sigmoid(x)


def _rms(x, g):
    ms = jnp.mean(x * x, axis=-1, keepdims=True)
    return x * lax.rsqrt(ms + RMS_EPS) * g


def _rmsnorm_kernel(xp_ref, xs_ref, g_ref, op_ref, os_ref, *, n_p):
    i = pl.program_id(0)

    @pl.when(i < n_p)
    def _():
        op_ref[...] = _rms(xp_ref[...], g_ref[...]).astype(op_ref.dtype)

    @pl.when(i == n_p)
    def _():
        os_ref[...] = _rms(xs_ref[...], g_ref[...]).astype(os_ref.dtype)


def _rmsnorm2(xp, xs, g, out_dtype, *, tm=1024):
    d = xp.shape[1]
    n_p = xp.shape[0] // tm
    pspec = pl.BlockSpec((tm, d), lambda i: (jnp.minimum(i, n_p - 1), 0))
    sspec = pl.BlockSpec((T_S, d), lambda i: (0, 0))
    return pl.pallas_call(
        functools.partial(_rmsnorm_kernel, n_p=n_p),
        grid=(n_p + 1,),
        in_specs=[pspec, sspec, pl.BlockSpec((1, d), lambda i: (0, 0))],
        out_specs=[pspec, sspec],
        out_shape=[jax.ShapeDtypeStruct(xp.shape, out_dtype), jax.ShapeDtypeStruct(xs.shape, out_dtype)],
        compiler_params=_cparams("arbitrary"),
        name="rmsnorm",
    )(xp, xs, g.reshape(1, d))


def _rmsnorm1_kernel(x_ref, g_ref, o_ref):
    o_ref[...] = _rms(x_ref[...], g_ref[...]).astype(o_ref.dtype)


def _rmsnorm1(x, g, out_dtype, *, tm):
    rows, d = x.shape
    return pl.pallas_call(
        _rmsnorm1_kernel,
        grid=(rows // tm,),
        in_specs=[pl.BlockSpec((tm, d), lambda i: (i, 0)), pl.BlockSpec((1, d), lambda i: (0, 0))],
        out_specs=pl.BlockSpec((tm, d), lambda i: (i, 0)),
        out_shape=jax.ShapeDtypeStruct((rows, d), out_dtype),
        compiler_params=_cparams("arbitrary"),
        name="rmsnorm_mem",
    )(x, g.reshape(1, d))


def _mm_kernel(*refs, has_s, has_res, n_w, epilogue):
    it = iter(refs)
    xp_ref = next(it)
    xs_ref = next(it) if has_s else None
    w_refs = [next(it) for _ in range(n_w)]
    rp_ref = next(it) if has_res else None
    rs_ref = next(it) if (has_res and has_s) else None
    op_ref = next(it)
    os_ref = next(it) if has_s else None
    wb_refs = [next(it) for _ in range(n_w)]

    def run(x_ref, r_ref, o_ref):
        x = x_ref[...]
        accs = [jnp.dot(x, wb_ref[...], preferred_element_type=F32) for wb_ref in wb_refs]
        out = epilogue(*accs)
        if r_ref is not None:
            out = out + r_ref[...]
        o_ref[...] = out.astype(o_ref.dtype)

    @pl.when(pl.program_id(1) == 0)
    def _():
        for w_ref, wb_ref in zip(w_refs, wb_refs):
            wb_ref[...] = w_ref[...].astype(BF16)
        if has_s:
            run(xs_ref, rs_ref, os_ref)

    run(xp_ref, rp_ref, op_ref)


def _identity(a):
    return a


def _swiglu_epilogue(a, u):
    return _silu(a) * u


def _matmul2(xp, xs, ws, layer, *, tn, tm, n_out=None, res=None, out_dtype=F32, epilogue=_identity):
    m, k = xp.shape
    n = ws[0].shape[2] if n_out is None else n_out
    has_s = xs is not None
    xp_spec = pl.BlockSpec((tm, k), lambda j, i: (i, 0))
    xs_spec = pl.BlockSpec((T_S, k), lambda j, i: (0, 0))
    w_spec = pl.BlockSpec((None, k, tn), lambda j, i: (layer, 0, j))
    op_spec = pl.BlockSpec((tm, tn), lambda j, i: (i, j))
    os_spec = pl.BlockSpec((T_S, tn), lambda j, i: (0, j))

    in_specs, args = [xp_spec], [xp]
    if has_s:
        in_specs.append(xs_spec)
        args.append(xs)
    in_specs += [w_spec] * len(ws)
    args += list(ws)
    if res is not None:
        in_specs.append(op_spec)
        args.append(res[0])
        if has_s:
            in_specs.append(os_spec)
            args.append(res[1])
    out_specs = [op_spec] + ([os_spec] if has_s else [])
    out_shape = [jax.ShapeDtypeStruct((m, n), out_dtype)] + ([jax.ShapeDtypeStruct((T_S, n), out_dtype)] if has_s else [])
    outs = pl.pallas_call(
        functools.partial(_mm_kernel, has_s=has_s, has_res=res is not None, n_w=len(ws), epilogue=epilogue),
        grid=(n // tn, m // tm),
        in_specs=in_specs,
        out_specs=out_specs,
        out_shape=out_shape,
        scratch_shapes=[pltpu.VMEM((k, tn), BF16) for _ in ws],
        compiler_params=_cparams("arbitrary", "arbitrary"),
        name="matmul",
    )(*args)
    return tuple(outs) if has_s else outs[0]


def _shift_rows(u, k, row):
    return jnp.where(row >= k, pltpu.roll(u, k, axis=0), 0.0)


def _sc_prompt_kernel(bg_ref, cg_ref, v_ref, w_ref, y_ref, hist_ref):
    u = cg_ref[...] * v_ref[...]
    row = lax.broadcasted_iota(jnp.int32, u.shape, 0)
    w = w_ref[...]
    conv = w[2:3, :] * u + w[1:2, :] * _shift_rows(u, 1, row) + w[0:1, :] * _shift_rows(u, 2, row)
    y_ref[...] = (bg_ref[...] * conv).astype(y_ref.dtype)
    hist_ref[...] = u[SEQ - 2:, :]


def _sc_prompt(bcv, w_conv, *, tc=256):
    nb = D_MODEL // tc
    return pl.pallas_call(
        _sc_prompt_kernel,
        grid=(BATCH, nb),
        in_specs=[pl.BlockSpec((SEQ, tc), lambda b, j: (b, j)),
                  pl.BlockSpec((SEQ, tc), lambda b, j: (b, j + nb)),
                  pl.BlockSpec((SEQ, tc), lambda b, j: (b, j + 2 * nb)),
                  pl.BlockSpec((3, tc), lambda b, j: (0, j))],
        out_specs=[pl.BlockSpec((SEQ, tc), lambda b, j: (b, j)),
                   pl.BlockSpec((None, 2, tc), lambda b, j: (b, 0, j))],
        out_shape=[jax.ShapeDtypeStruct((T_P, D_MODEL), BF16),
                   jax.ShapeDtypeStruct((BATCH, 2, D_MODEL), F32)],
        compiler_params=_cparams("arbitrary", "arbitrary"),
        name="sc_prompt",
    )(bcv, bcv, bcv, w_conv)


def _sc_sample_kernel(bg_ref, cg_ref, v_ref, h0_ref, h1_ref, w_ref, y_ref, u_ref):
    u = cg_ref[...] * v_ref[...]
    w = w_ref[...]
    conv = w[2:3, :] * u + w[1:2, :] * h1_ref[...] + w[0:1, :] * h0_ref[...]
    y_ref[...] = (bg_ref[...] * conv).astype(y_ref.dtype)
    u_ref[...] = u


def _sc_sample(bcv, h0, h1, w_conv, *, tc=512):
    nb = D_MODEL // tc
    hspec = pl.BlockSpec((T_S, tc), lambda j: (0, j))
    return pl.pallas_call(
        _sc_sample_kernel,
        grid=(nb,),
        in_specs=[hspec,
                  pl.BlockSpec((T_S, tc), lambda j: (0, j + nb)),
                  pl.BlockSpec((T_S, tc), lambda j: (0, j + 2 * nb)),
                  hspec, hspec,
                  pl.BlockSpec((3, tc), lambda j: (0, j))],
        out_specs=[hspec, hspec],
        out_shape=[jax.ShapeDtypeStruct((T_S, D_MODEL), BF16),
                   jax.ShapeDtypeStruct((T_S, D_MODEL), F32)],
        compiler_params=_cparams("arbitrary"),
        name="sc_sample",
    )(bcv, bcv, bcv, h0, h1, w_conv)


def _ssdconv_sample_kernel(x_ref, h0_ref, h1_ref, h2_ref, w_ref, b_ref, o_ref, raw_ref):
    x = x_ref[...]
    w = w_ref[...]
    conv = w[3:4, :] * x + w[2:3, :] * h2_ref[...] + w[1:2, :] * h1_ref[...] + w[0:1, :] * h0_ref[...]
    o_ref[...] = _silu(conv + b_ref[...])
    raw_ref[...] = x


def _ssdconv_sample(zx, h0, h1, h2, conv_w, conv_b, *, tc=512):
    nb = SSD_CONV_DIM // tc
    off = D_INNER // tc
    hspec = pl.BlockSpec((T_S, tc), lambda j: (0, j))
    return pl.pallas_call(
        _ssdconv_sample_kernel,
        grid=(nb,),
        in_specs=[pl.BlockSpec((T_S, tc), lambda j: (0, j + off)), hspec, hspec, hspec,
                  pl.BlockSpec((4, tc), lambda j: (0, j)),
                  pl.BlockSpec((1, tc), lambda j: (0, j))],
        out_specs=[hspec, hspec],
        out_shape=[jax.ShapeDtypeStruct((T_S, SSD_CONV_DIM), F32),
                   jax.ShapeDtypeStruct((T_S, SSD_CONV_DIM), F32)],
        compiler_params=_cparams("arbitrary"),
        name="ssdconv_sample",
    )(zx, h0, h1, h2, conv_w, conv_b)


def _dt_kernel(xp_ref, xs_ref, w_ref, bias_ref, alog_ref, dtp_ref, csp_ref, dts_ref, decs_ref, *, n_p):
    i = pl.program_id(0)

    def dt_da(x_ref):
        w = w_ref[:, :SSD_HEADS].astype(BF16)
        raw = jnp.dot(x_ref[...], w, preferred_element_type=F32) + bias_ref[...]
        dt = jnp.maximum(raw, 0.0) + jnp.log1p(jnp.exp(-jnp.abs(raw)))
        return dt, dt * (-jnp.exp(alog_ref[...]))

    @pl.when(i < n_p)
    def _():
        dt, da = dt_da(xp_ref)
        dtp_ref[...] = dt
        row = lax.broadcasted_iota(jnp.int32, da.shape, 0)
        cs = da
        k = 1
        while k < SSD_CHUNK:
            cs = cs + jnp.where(row >= k, pltpu.roll(cs, k, axis=0), 0.0)
            k *= 2
        csp_ref[...] = cs

    @pl.when(i == n_p)
    def _():
        dt, da = dt_da(xs_ref)
        dts_ref[...] = dt
        decs_ref[...] = jnp.exp(da)


def _dt_prep(hn_p, hn_s, w_in3, layer, dt_bias, a_log):
    k = hn_p.shape[1]
    n_p = T_P // SSD_CHUNK
    dt_block = (D_INNER + SSD_CONV_DIM) // LANES
    pin = pl.BlockSpec((SSD_CHUNK, k), lambda i: (jnp.minimum(i, n_p - 1), 0))
    pout = pl.BlockSpec((SSD_CHUNK, SSD_HEADS), lambda i: (jnp.minimum(i, n_p - 1), 0))
    sout = pl.BlockSpec((T_S, SSD_HEADS), lambda i: (0, 0))
    one = pl.BlockSpec((1, SSD_HEADS), lambda i: (0, 0))
    return pl.pallas_call(
        functools.partial(_dt_kernel, n_p=n_p),
        grid=(n_p + 1,),
        in_specs=[pin, pl.BlockSpec((T_S, k), lambda i: (0, 0)),
                  pl.BlockSpec((None, k, LANES), lambda i: (layer, 0, dt_block)), one, one],
        out_specs=[pout, pout, sout, sout],
        out_shape=[jax.ShapeDtypeStruct((T_P, SSD_HEADS), F32), jax.ShapeDtypeStruct((T_P, SSD_HEADS), F32),
                   jax.ShapeDtypeStruct((T_S, SSD_HEADS), F32), jax.ShapeDtypeStruct((T_S, SSD_HEADS), F32)],
        compiler_params=_cparams("arbitrary"),
        name="ssd_dt",
    )(hn_p, hn_s, w_in3, dt_bias.reshape(1, SSD_HEADS), a_log.reshape(1, SSD_HEADS))


def _expand_heads(v, onehot):
    hi = v.astype(BF16)
    lo = (v - hi.astype(F32)).astype(BF16)
    return (jnp.dot(hi, onehot, preferred_element_type=F32)
            + jnp.dot(lo, onehot, preferred_element_type=F32))


def _gate_norm(y, z, ng):
    return _rms(y * _silu(z), ng)


def _ssd_group_chunk(x, bm, cm, z, dtc, csc, dtr, csr, dsk, ng, st):
    q = SSD_CHUNK
    xb = x.astype(BF16)
    bb = bm.astype(BF16)
    cb = cm.astype(BF16)

    lane = lax.broadcasted_iota(jnp.int32, (SSD_HPG, GROUP_W), 1)
    head = lax.broadcasted_iota(jnp.int32, (SSD_HPG, GROUP_W), 0)
    onehot = jnp.where(lane // SSD_HEAD_DIM == head, 1.0, 0.0).astype(BF16)

    cbm = lax.dot_general(cb, bb, (((1,), (1,)), ((), ())), preferred_element_type=F32)
    row = lax.broadcasted_iota(jnp.int32, (q, q), 0)
    col = lax.broadcasted_iota(jnp.int32, (q, q), 1)
    tri = row >= col
    lane128 = lax.broadcasted_iota(jnp.int32, (q, LANES), 1)
    lo_half = lane128 < SSD_HEAD_DIM

    tiles = []
    for m in range(GROUP_W // LANES):
        xt = xb[:, LANES * m:LANES * (m + 1)]
        acc = None
        for j, xm in ((2 * m, jnp.where(lo_half, xt, 0)), (2 * m + 1, jnp.where(lo_half, 0, xt))):
            seg = csc[:, j:j + 1] - csr[j:j + 1, :]
            dec = jnp.exp(jnp.where(tri, seg, -jnp.inf))
            w = (cbm * dec * dtr[j:j + 1, :]).astype(BF16)
            part = jnp.dot(w, xm, preferred_element_type=F32)
            acc = part if acc is None else acc + part
        tiles.append(acc)
    y_diag = jnp.concatenate(tiles, axis=1)

    e_exp = _expand_heads(jnp.exp(csc), onehot)
    y_off = jnp.dot(cb, st.astype(BF16), preferred_element_type=F32) * e_exp

    cs_last = csc[q - 1:q, :]
    te = _expand_heads(jnp.exp(cs_last - csc) * dtc, onehot)
    xw = (x * te).astype(BF16)
    d_st = jnp.dot(bm.T.astype(BF16), xw, preferred_element_type=F32)
    st_new = st * e_exp[q - 1:q, :] + d_st
    y = y_diag + y_off + dsk * x
    return _gate_norm(y, z, ng), st_new


def _causal_conv4_silu(x, prev8, w, bias):
    row8 = lax.broadcasted_iota(jnp.int32, prev8.shape, 0)
    acc = w[3:4, :] * x + bias
    for k in (1, 2, 3):
        r = pltpu.roll(x, k, axis=0)
        top = jnp.where(row8 < k, pltpu.roll(prev8, k, axis=0), r[0:8, :])
        acc = acc + w[3 - k:4 - k, :] * jnp.concatenate([top, r[8:, :]], axis=0)
    return _silu(acc)


def _ssd_scan_kernel(x_ref, b_ref, c_ref, z_ref, wx_ref, wb_ref, wc_ref, bx_ref, bb_ref, bc_ref,
                     dtc_ref, csc_ref, dtr_ref, csr_ref, dsk_ref, ng_ref,
                     y_ref, sfin_ref, hist_ref, st_ref, px_ref, pb_ref, pc_ref):
    q = SSD_CHUNK
    c = pl.program_id(1)

    @pl.when(c == 0)
    def _():
        st_ref[...] = jnp.zeros_like(st_ref)
        px_ref[...] = jnp.zeros_like(px_ref)
        pb_ref[...] = jnp.zeros_like(pb_ref)
        pc_ref[...] = jnp.zeros_like(pc_ref)

    for gi in range(SSD_GROUPS):
        xs = slice(GROUP_W * gi, GROUP_W * (gi + 1))
        ns = slice(SSD_STATE * gi, SSD_STATE * (gi + 1))
        x = _causal_conv4_silu(x_ref[:, xs], px_ref[:, xs], wx_ref[:, xs], bx_ref[:, xs])
        bm = _causal_conv4_silu(b_ref[:, ns], pb_ref[:, ns], wb_ref[:, ns], bb_ref[:, ns])
        cm = _causal_conv4_silu(c_ref[:, ns], pc_ref[:, ns], wc_ref[:, ns], bc_ref[:, ns])
        y, st_new = _ssd_group_chunk(x, bm, cm, z_ref[:, xs],
                                     dtc_ref[gi], csc_ref[gi], dtr_ref[gi], csr_ref[gi],
                                     dsk_ref[gi], ng_ref[gi], st_ref[gi])
        y_ref[:, xs] = y.astype(y_ref.dtype)
        st_ref[gi] = st_new

    px_ref[...] = x_ref[q - 8:, :]
    pb_ref[...] = b_ref[q - 8:, :]
    pc_ref[...] = c_ref[q - 8:, :]

    @pl.when(c == pl.num_programs(1) - 1)
    def _():
        for gi in range(SSD_GROUPS):
            sfin_ref[GROUP_W * gi:GROUP_W * (gi + 1), :] = st_ref[gi].T
        hist_ref[:, 0:D_INNER] = x_ref[q - 3:, :]
        hist_ref[:, D_INNER:D_INNER + BC_W] = b_ref[q - 3:, :]
        hist_ref[:, D_INNER + BC_W:] = c_ref[q - 3:, :]


def _ssd_scan_prompt(zx, conv_w, conv_b, dtc, csc, dtr, csr, dskip, ng):
    q = SSD_CHUNK
    nc = SEQ // q
    b_blk = D_INNER // BC_W
    small_c = pl.BlockSpec((SSD_GROUPS, q, SSD_HPG), lambda b, c: (0, b * nc + c, 0))
    small_r = pl.BlockSpec((SSD_GROUPS, SSD_HPG, q), lambda b, c: (0, 0, b * nc + c))
    pspec = pl.BlockSpec((SSD_GROUPS, 1, GROUP_W), lambda b, c: (0, 0, 0))

    def rows(width, col_block):
        return pl.BlockSpec((q, width), lambda b, c: (b * nc + c, col_block))

    def par(nrows, width, col_block):
        return pl.BlockSpec((nrows, width), lambda b, c: (0, col_block))

    return pl.pallas_call(
        _ssd_scan_kernel,
        grid=(BATCH, nc),
        in_specs=[rows(D_INNER, 1), rows(BC_W, 2 * b_blk), rows(BC_W, 2 * b_blk + 1), rows(D_INNER, 0),
                  par(4, D_INNER, 0), par(4, BC_W, b_blk), par(4, BC_W, b_blk + 1),
                  par(1, D_INNER, 0), par(1, BC_W, b_blk), par(1, BC_W, b_blk + 1),
                  small_c, small_c, small_r, small_r, pspec, pspec],
        out_specs=[rows(D_INNER, 0),
                   pl.BlockSpec((D_INNER, SSD_STATE), lambda b, c: (b, 0)),
                   pl.BlockSpec((None, 3, SSD_CONV_DIM), lambda b, c: (b, 0, 0))],
        out_shape=[jax.ShapeDtypeStruct((T_P, D_INNER), BF16),
                   jax.ShapeDtypeStruct((BATCH * D_INNER, SSD_STATE), F32),
                   jax.ShapeDtypeStruct((BATCH, 3, SSD_CONV_DIM), F32)],
        scratch_shapes=[pltpu.VMEM((SSD_GROUPS, SSD_STATE, GROUP_W), F32),
                        pltpu.VMEM((8, D_INNER), F32), pltpu.VMEM((8, BC_W), F32), pltpu.VMEM((8, BC_W), F32)],
        compiler_params=_cparams("arbitrary", "arbitrary"),
        name="ssd_scan",
    )(zx, zx, zx, zx, conv_w, conv_w, conv_w, conv_b, conv_b, conv_b, dtc, csc, dtr, csr, dskip, ng)


def _ssd_step_kernel(s_ref, xt_ref, dtt_ref, bt_ref, ce_ref, dec_ref, so_ref, yt_ref):
    xdt = (xt_ref[...] * dtt_ref[...]).astype(BF16)
    tile_of_lane = lax.broadcasted_iota(jnp.int32, (N_TILES, D_INNER), 1) // LANES
    row32 = lax.broadcasted_iota(jnp.int32, (N_TILES, D_INNER), 0)
    b_big = jnp.where(tile_of_lane == row32, bt_ref[...], 0.0).astype(BF16)
    upd = jnp.dot(xdt, b_big, preferred_element_type=F32)

    lane32 = lax.broadcasted_iota(jnp.int32, (SSD_STATE, N_TILES), 1)
    ce = ce_ref[...]
    s_tiles, c_tiles = [], []
    for r in range(N_TILES):
        dtile = jnp.concatenate(
            [jnp.broadcast_to(dec_ref[2 * r:2 * r + 1, :], (SSD_HEAD_DIM, SSD_STATE)),
             jnp.broadcast_to(dec_ref[2 * r + 1:2 * r + 2, :], (SSD_HEAD_DIM, SSD_STATE))], axis=0)
        s_n = s_ref[LANES * r:LANES * (r + 1), :] * dtile + upd[:, LANES * r:LANES * (r + 1)]
        so_ref[LANES * r:LANES * (r + 1), :] = s_n
        s_tiles.append(s_n.astype(BF16))
        c_tiles.append(jnp.where(lane32 == r, ce, 0.0).astype(BF16))
    s_big = jnp.concatenate(s_tiles, axis=1)
    c_big = jnp.concatenate(c_tiles, axis=0)
    yt_ref[...] = jnp.dot(s_big, c_big, preferred_element_type=F32)


def _ssd_step_sample(state, x_t, dt_t, b_tiled, c_exp, dec):
    tspec = pl.BlockSpec((None, LANES, N_TILES), lambda b: (b, 0, 0))
    sspec = pl.BlockSpec((None, D_INNER, SSD_STATE), lambda b: (b, 0, 0))
    return pl.pallas_call(
        _ssd_step_kernel,
        grid=(DEC_BATCH,),
        in_specs=[sspec, tspec, tspec,
                  pl.BlockSpec((None, 1, D_INNER), lambda b: (b, 0, 0)),
                  pl.BlockSpec((None, SSD_STATE, N_TILES), lambda b: (b, 0, 0)),
                  pl.BlockSpec((None, SSD_HEADS, SSD_STATE), lambda b: (b, 0, 0))],
        out_specs=[sspec, tspec],
        out_shape=[jax.ShapeDtypeStruct((DEC_BATCH, D_INNER, SSD_STATE), F32),
                   jax.ShapeDtypeStruct((DEC_BATCH, LANES, N_TILES), F32)],
        compiler_params=_cparams("arbitrary"),
        name="ssd_step",
    )(state, x_t, dt_t, b_tiled, c_exp, dec)


def _gate_norm_kernel(y_ref, x_ref, z_ref, dsk_ref, ng_ref, o_ref):
    y = y_ref[...] + dsk_ref[...] * x_ref[...]
    o_ref[...] = _gate_norm(y, z_ref[...], ng_ref[...]).astype(o_ref.dtype)


def _gate_norm_sample(y_s, xbc_c_s, zx_s, dskip, ng):
    gspec = pl.BlockSpec((T_S, GROUP_W), lambda g: (0, g))
    pspec = pl.BlockSpec((None, 1, GROUP_W), lambda g: (g, 0, 0))
    return pl.pallas_call(
        _gate_norm_kernel,
        grid=(SSD_GROUPS,),
        in_specs=[gspec, gspec, gspec, pspec, pspec],
        out_specs=gspec,
        out_shape=jax.ShapeDtypeStruct((T_S, D_INNER), BF16),
        compiler_params=_cparams("arbitrary"),
        name="ssd_gate_norm",
    )(y_s, xbc_c_s, zx_s, dskip, ng)


def _xattn_prompt_kernel(q_ref, k_ref, v_ref, o_ref):
    scale = MEM_HEAD_DIM ** -0.5
    for h in range(MEM_HEADS):
        sl = slice(MEM_HEAD_DIM * h, MEM_HEAD_DIM * (h + 1))
        kh = k_ref[:, sl].astype(BF16)
        vh = v_ref[:, sl].astype(BF16)
        s = lax.dot_general(q_ref[:, sl], kh, (((1,), (1,)), ((), ())), preferred_element_type=F32) * scale
        m = jnp.max(s, axis=-1, keepdims=True)
        e = jnp.exp(s - m)
        p = (e / jnp.sum(e, axis=-1, keepdims=True)).astype(BF16)
        o_ref[:, sl] = jnp.dot(p, vh, preferred_element_type=F32).astype(o_ref.dtype)


def _xattn_prompt(q, k, v, *, tq=512):
    nq = SEQ // tq
    kspec = pl.BlockSpec((MEM_LEN, D_MODEL), lambda b, i: (b, 0))
    return pl.pallas_call(
        _xattn_prompt_kernel,
        grid=(BATCH, nq),
        in_specs=[pl.BlockSpec((tq, D_MODEL), lambda b, i: (b * nq + i, 0)), kspec, kspec],
        out_specs=pl.BlockSpec((tq, D_MODEL), lambda b, i: (b * nq + i, 0)),
        out_shape=jax.ShapeDtypeStruct((T_P, D_MODEL), BF16),
        compiler_params=_cparams("arbitrary", "arbitrary"),
        name="xattn_prompt",
    )(q, k, v)


def _xattn_sample_kernel(q_ref, k_ref, v_ref, o_ref, *, bb):
    scale = MEM_HEAD_DIM ** -0.5
    for b in range(bb):
        for h in range(MEM_HEADS):
            kh = k_ref[b, :, h, :]
            s = jnp.sum(kh * q_ref[b, h:h + 1, :], axis=-1, keepdims=True) * scale
            m = jnp.max(s, axis=0, keepdims=True)
            e = jnp.exp(s - m)
            p = e / jnp.sum(e, axis=0, keepdims=True)
            o_ref[b, h:h + 1, :] = jnp.sum(p * v_ref[b, :, h, :], axis=0, keepdims=True)


def _xattn_sample(q_s, kc, vc, layer, *, bb=2):
    cspec = pl.BlockSpec((None, bb, MEM_LEN, MEM_HEADS, MEM_HEAD_DIM), lambda i: (layer, i, 0, 0, 0))
    qspec = pl.BlockSpec((bb, MEM_HEADS, MEM_HEAD_DIM), lambda i: (i, 0, 0))
    return pl.pallas_call(
        functools.partial(_xattn_sample_kernel, bb=bb),
        grid=(DEC_BATCH // bb,),
        in_specs=[qspec, cspec, cspec],
        out_specs=qspec,
        out_shape=jax.ShapeDtypeStruct((DEC_BATCH, MEM_HEADS, MEM_HEAD_DIM), F32),
        compiler_params=_cparams("arbitrary"),
        name="xattn_sample",
    )(q_s, kc, vc)


def _mem_attn_block(hp, hs, i, norm_mem_q, xa_w_q, xa_w_o, k_p, v_p, kc, vc):
    hq_p, hq_s = _rmsnorm2(hp, hs, norm_mem_q[i], BF16)
    q_p, q_s = _matmul2(hq_p, hq_s, [xa_w_q], i, tn=1024, tm=1024, out_dtype=BF16)
    o_p = _xattn_prompt(q_p, k_p, v_p)
    o_s = _xattn_sample(q_s.astype(F32).reshape(DEC_BATCH, MEM_HEADS, MEM_HEAD_DIM), kc, vc, i)
    o_s = o_s.reshape(DEC_BATCH, D_MODEL).astype(BF16)
    return _matmul2(o_p, o_s, [xa_w_o], i, tn=1024, tm=1024, res=(hp, hs))


def _ffn_block(hp, hs, i, norm_ffn, w_gate, w_up, w_down):
    hf_p, hf_s = _rmsnorm2(hp, hs, norm_ffn[i], BF16)
    a_p, a_s = _matmul2(hf_p, hf_s, [w_gate, w_up], i, tn=512, tm=1024, out_dtype=BF16, epilogue=_swiglu_epilogue)
    return _matmul2(a_p, a_s, [w_down], i, tn=512, tm=512, res=(hp, hs))


def _short_conv_block(hp, hs, a, norm_g, cache_sc, sc_w_in, sc_w_conv, sc_w_out):
    hn_p, hn_s = _rmsnorm2(hp, hs, norm_g, BF16)
    bcv_p, bcv_s = _matmul2(hn_p, hn_s, [sc_w_in], a, tn=1024, tm=1024)
    w_conv = sc_w_conv[a]
    y_p, hist_p = _sc_prompt(bcv_p, w_conv)
    h0, h1 = cache_sc[a, :, 0, :], cache_sc[a, :, 1, :]
    y_s, u_s = _sc_sample(bcv_s, h0, h1, w_conv)
    hp, hs = _matmul2(y_p, y_s, [sc_w_out], a, tn=1024, tm=1024, res=(hp, hs))
    return hp, hs, hist_p, jnp.stack([h1, u_s], axis=1)


def _ssd_block(hp, hs, j, norm_g, conv_hist, state, ssd_w_in, conv_w, conv_b, dt_bias, a_log, d_skip, norm_ssd, w_out):
    hn_p, hn_s = _rmsnorm2(hp, hs, norm_g, BF16)
    zx_p, zx_s = _matmul2(hn_p, hn_s, [ssd_w_in], j, tn=1024, tm=1024, n_out=D_INNER + SSD_CONV_DIM)

    dt_p, cs_p, dt_s, dec_s = _dt_prep(hn_p, hn_s, ssd_w_in, j, dt_bias[j], a_log[j])

    cw, cbias = conv_w[j], conv_b[j].reshape(1, SSD_CONV_DIM)
    ch = conv_hist[j]
    xbc_c_s, raw_s = _ssdconv_sample(zx_s, ch[:, 0, :], ch[:, 1, :], ch[:, 2, :], cw, cbias)
    chist_s = jnp.stack([ch[:, 1, :], ch[:, 2, :], raw_s], axis=1)

    dskip = jnp.repeat(d_skip[j], SSD_HEAD_DIM).reshape(SSD_GROUPS, 1, GROUP_W)
    ng = norm_ssd[j].reshape(SSD_GROUPS, 1, GROUP_W)

    def per_group(v):
        v = v[:, :SSD_HEADS].reshape(T_P, SSD_GROUPS, SSD_HPG)
        return jnp.transpose(v, (1, 0, 2)), jnp.transpose(v, (1, 2, 0))

    dtc, dtr = per_group(dt_p)
    csc, csr = per_group(cs_p)
    g_p, sfin, chist_p = _ssd_scan_prompt(zx_p, cw, cbias, dtc, csc, dtr, csr, dskip, ng)
    ss_p = sfin.reshape(BATCH, SSD_HEADS, SSD_HEAD_DIM, SSD_STATE)

    def tile_major(v):
        return jnp.transpose(v.reshape(DEC_BATCH, N_TILES, LANES), (0, 2, 1))

    tiles_per_group = GROUP_W // LANES
    x_t = tile_major(xbc_c_s[:, :D_INNER])
    dt_t = tile_major(jnp.repeat(dt_s[:, :SSD_HEADS], SSD_HEAD_DIM, axis=1))
    b_g = xbc_c_s[:, D_INNER:D_INNER + BC_W].reshape(DEC_BATCH, SSD_GROUPS, SSD_STATE)
    c_g = xbc_c_s[:, D_INNER + BC_W:].reshape(DEC_BATCH, SSD_GROUPS, SSD_STATE)
    b_tiled = jnp.repeat(b_g, tiles_per_group, axis=1).reshape(DEC_BATCH, 1, D_INNER)
    c_exp = jnp.transpose(jnp.repeat(c_g, tiles_per_group, axis=1), (0, 2, 1))
    dec_b = jnp.broadcast_to(dec_s[:, :SSD_HEADS, None], (DEC_BATCH, SSD_HEADS, SSD_STATE))
    s_new, y_t = _ssd_step_sample(state[j].reshape(DEC_BATCH, D_INNER, SSD_STATE), x_t, dt_t, b_tiled, c_exp, dec_b)
    y_s = jnp.transpose(y_t, (0, 2, 1)).reshape(DEC_BATCH, D_INNER)
    g_s = _gate_norm_sample(y_s, xbc_c_s, zx_s, dskip, ng)
    ss_s = s_new.reshape(DEC_BATCH, SSD_HEADS, SSD_HEAD_DIM, SSD_STATE)

    hp, hs = _matmul2(g_p, g_s, [w_out], j, tn=512, tm=1024, res=(hp, hs))
    return hp, hs, chist_p, chist_s, ss_p, ss_s


def kernel(x_prompt, x_sample, mem_prompt, cache_sc, state_ssd_conv, state_ssd, cache_mem_k, cache_mem_v, norm_mix, norm_mem_q, norm_mem_kv, norm_ffn, norm_final, sc_w_in, sc_w_conv, sc_w_out, ssd_w_in, ssd_conv_w, ssd_conv_b, ssd_dt_bias, ssd_a_log, ssd_d, ssd_norm, ssd_w_out, xa_w_q, xa_w_k, xa_w_v, xa_w_o, ffn_w_gate, ffn_w_up, ffn_w_down):
    depth = norm_mix.shape[0]
    hp = x_prompt.reshape(T_P, D_MODEL)
    hs = x_sample.reshape(T_S, D_MODEL)

    mem = mem_prompt.reshape(BATCH * MEM_LEN, D_MODEL)
    k_ps, v_ps = [], []
    for i in range(depth):
        m = _rmsnorm1(mem, norm_mem_kv[i], BF16, tm=512)
        k_ps.append(_matmul2(m, None, [xa_w_k], i, tn=1024, tm=512))
        v_ps.append(_matmul2(m, None, [xa_w_v], i, tn=1024, tm=512))

    sc_p, sc_s, sconv_p, sconv_s, ss_p, ss_s = [], [], [], [], [], []
    for i in range(depth):
        if i % 2 == 0:
            hp, hs, a_p, a_s = _short_conv_block(hp, hs, i // 2, norm_mix[i], cache_sc, sc_w_in, sc_w_conv, sc_w_out)
            sc_p.append(a_p)
            sc_s.append(a_s)
        else:
            hp, hs, c_p, c_s, s_p, s_s = _ssd_block(hp, hs, i // 2, norm_mix[i], state_ssd_conv, state_ssd, ssd_w_in,
                                                    ssd_conv_w, ssd_conv_b, ssd_dt_bias, ssd_a_log, ssd_d, ssd_norm,
                                                    ssd_w_out)
            sconv_p.append(c_p)
            sconv_s.append(c_s)
            ss_p.append(s_p)
            ss_s.append(s_s)
        hp, hs = _mem_attn_block(hp, hs, i, norm_mem_q, xa_w_q, xa_w_o, k_ps[i], v_ps[i], cache_mem_k, cache_mem_v)
        hp, hs = _ffn_block(hp, hs, i, norm_ffn, ffn_w_gate, ffn_w_up, ffn_w_down)

    y_p, y_s = _rmsnorm2(hp, hs, norm_final, F32)

    kv_shape = (depth, BATCH, MEM_LEN, MEM_HEADS, MEM_HEAD_DIM)
    return (y_p.reshape(BATCH, SEQ, D_MODEL), y_s.reshape(DEC_BATCH, 1, D_MODEL),
            jnp.stack(sc_p), jnp.stack(sc_s), jnp.stack(sconv_p), jnp.stack(sconv_s),
            jnp.stack(ss_p), jnp.stack(ss_s),
            jnp.stack(k_ps).reshape(kv_shape), jnp.stack(v_ps).reshape(kv_shape))
```

```python
import functools

import jax
import jax.numpy as jnp
from jax import lax
from jax.experimental import pallas as pl
from jax.experimental.pallas import tpu as pltpu

F32 = jnp.float32
BF16 = jnp.bfloat16

D_MODEL = 2048
BATCH = 4
SEQ = 2048
DEC_BATCH = 128
T_P = BATCH * SEQ
T_S = DEC_BATCH
D_INNER = 4096
SSD_HEAD_DIM = 64
SSD_HEADS = 64
SSD_GROUPS = 8
SSD_HPG = 8
SSD_STATE = 128
GROUP_W = SSD_HPG * SSD_HEAD_DIM
BC_W = SSD_GROUPS * SSD_STATE
SSD_CONV_DIM = D_INNER + 2 * BC_W
MEM_LEN = 256
MEM_HEADS = 4
MEM_HEAD_DIM = 512
RMS_EPS = 1e-5
LANES = 128

V7X_VMEM_LIMIT_BYTES = 56 * 1024 * 1024
SSD_CHUNK = 128
SSD_STEP_SEQS = 2
N_TILES = D_INNER // LANES


def _cparams(*sem):
    return pltpu.CompilerParams(dimension_semantics=sem, vmem_limit_bytes=V7X_VMEM_LIMIT_BYTES)


def _silu(x):
    return x * jax.nn.sigmoid(x)


def _rms(x, g):
    ms = jnp.mean(x * x, axis=-1, keepdims=True)
    return x * lax.rsqrt(ms + RMS_EPS) * g


def _rmsnorm_kernel(xp_ref, xs_ref, g_ref, op_ref, os_ref, *, n_p):
    i = pl.program_id(0)

    @pl.when(i < n_p)
    def _():
        op_ref[...] = _rms(xp_ref[...], g_ref[...]).astype(op_ref.dtype)

    @pl.when(i == n_p)
    def _():
        os_ref[...] = _rms(xs_ref[...], g_ref[...]).astype(os_ref.dtype)


def _rmsnorm2(xp, xs, g, out_dtype, *, tm=1024):
    d = xp.shape[1]
    n_p = xp.shape[0] // tm
    pspec = pl.BlockSpec((tm, d), lambda i: (jnp.minimum(i, n_p - 1), 0))
    sspec = pl.BlockSpec((T_S, d), lambda i: (0, 0))
    return pl.pallas_call(
        functools.partial(_rmsnorm_kernel, n_p=n_p),
        grid=(n_p + 1,),
        in_specs=[pspec, sspec, pl.BlockSpec((1, d), lambda i: (0, 0))],
        out_specs=[pspec, sspec],
        out_shape=[jax.ShapeDtypeStruct(xp.shape, out_dtype), jax.ShapeDtypeStruct(xs.shape, out_dtype)],
        compiler_params=_cparams("arbitrary"),
        name="rmsnorm",
    )(xp, xs, g.reshape(1, d))


def _rmsnorm1_kernel(x_ref, g_ref, o_ref):
    o_ref[...] = _rms(x_ref[...], g_ref[...]).astype(o_ref.dtype)


def _rmsnorm1(x, g, out_dtype, *, tm):
    rows, d = x.shape
    return pl.pallas_call(
        _rmsnorm1_kernel,
        grid=(rows // tm,),
        in_specs=[pl.BlockSpec((tm, d), lambda i: (i, 0)), pl.BlockSpec((1, d), lambda i: (0, 0))],
        out_specs=pl.BlockSpec((tm, d), lambda i: (i, 0)),
        out_shape=jax.ShapeDtypeStruct((rows, d), out_dtype),
        compiler_params=_cparams("arbitrary"),
        name="rmsnorm_mem",
    )(x, g.reshape(1, d))


_NT_DIMS = (((1,), (1,)), ((), ()))


def _mm_kernel(*refs, has_s, has_res, n_w, epilogue, w_transposed):
    it = iter(refs)
    xp_ref = next(it)
    xs_ref = next(it) if has_s else None
    w_refs = [next(it) for _ in range(n_w)]
    rp_ref = next(it) if has_res else None
    rs_ref = next(it) if (has_res and has_s) else None
    op_ref = next(it)
    os_ref = next(it) if has_s else None
    wb_refs = [next(it) for _ in range(n_w)]

    def run(x_ref, r_ref, o_ref):
        x = x_ref[...]
        if w_transposed:
            accs = [lax.dot_general(x, wb_ref[...], _NT_DIMS, preferred_element_type=F32) for wb_ref in wb_refs]
        else:
            accs = [jnp.dot(x, wb_ref[...], preferred_element_type=F32) for wb_ref in wb_refs]
        out = epilogue(*accs)
        if r_ref is not None:
            out = out + r_ref[...]
        o_ref[...] = out.astype(o_ref.dtype)

    @pl.when(pl.program_id(1) == 0)
    def _():
        for w_ref, wb_ref in zip(w_refs, wb_refs):
            wb_ref[...] = w_ref[...].astype(BF16)
        if has_s:
            run(xs_ref, rs_ref, os_ref)

    run(xp_ref, rp_ref, op_ref)


def _identity(a):
    return a


def _swiglu_epilogue(a, u):
    return _silu(a) * u


def _matmul2(xp, xs, ws, layer, *, tn, tm, n_out=None, res=None, out_dtype=F32, epilogue=_identity,
             w_transposed=False):
    m, k = xp.shape
    n = n_out if n_out is not None else ws[0].shape[1 if w_transposed else 2]
    has_s = xs is not None
    xp_spec = pl.BlockSpec((tm, k), lambda j, i: (i, 0))
    xs_spec = pl.BlockSpec((T_S, k), lambda j, i: (0, 0))
    if w_transposed:
        w_spec = pl.BlockSpec((None, tn, k), lambda j, i: (layer, j, 0))
    else:
        w_spec = pl.BlockSpec((None, k, tn), lambda j, i: (layer, 0, j))
    op_spec = pl.BlockSpec((tm, tn), lambda j, i: (i, j))
    os_spec = pl.BlockSpec((T_S, tn), lambda j, i: (0, j))

    in_specs, args = [xp_spec], [xp]
    if has_s:
        in_specs.append(xs_spec)
        args.append(xs)
    in_specs += [w_spec] * len(ws)
    args += list(ws)
    if res is not None:
        in_specs.append(op_spec)
        args.append(res[0])
        if has_s:
            in_specs.append(os_spec)
            args.append(res[1])
    out_specs = [op_spec] + ([os_spec] if has_s else [])
    out_shape = [jax.ShapeDtypeStruct((m, n), out_dtype)] + ([jax.ShapeDtypeStruct((T_S, n), out_dtype)] if has_s else [])
    outs = pl.pallas_call(
        functools.partial(_mm_kernel, has_s=has_s, has_res=res is not None, n_w=len(ws), epilogue=epilogue,
                          w_transposed=w_transposed),
        grid=(n // tn, m // tm),
        in_specs=in_specs,
        out_specs=out_specs,
        out_shape=out_shape,
        scratch_shapes=[pltpu.VMEM((tn, k) if w_transposed else (k, tn), BF16) for _ in ws],
        compiler_params=_cparams("arbitrary", "arbitrary"),
        name="matmul",
    )(*args)
    return tuple(outs) if has_s else outs[0]


def _shifted_rows(u, prev8, k):
    row8 = lax.broadcasted_iota(jnp.int32, prev8.shape, 0)
    r = pltpu.roll(u, k, axis=0)
    top = jnp.where(row8 < k, pltpu.roll(prev8, k, axis=0), r[0:8, :])
    return jnp.concatenate([top, r[8:, :]], axis=0)


def _sc_in_kernel(xp_ref, xs_ref, wb_ref, wc_ref, wv_ref, cw_ref, h0_ref, h1_ref,
                  yp_ref, ys_ref, us_ref, hist_ref, wbb_ref, wcb_ref, wvb_ref, prev_ref, *, tiles_per_seq):
    i = pl.program_id(1)
    cw = cw_ref[...]

    def gates(x):
        return (jnp.dot(x, wbb_ref[...], preferred_element_type=F32),
                jnp.dot(x, wcb_ref[...], preferred_element_type=F32),
                jnp.dot(x, wvb_ref[...], preferred_element_type=F32))

    @pl.when(i == 0)
    def _():
        wbb_ref[...] = wb_ref[...].astype(BF16)
        wcb_ref[...] = wc_ref[...].astype(BF16)
        wvb_ref[...] = wv_ref[...].astype(BF16)
        bg, cg, v = gates(xs_ref[...])
        u = cg * v
        conv = cw[2:3, :] * u + cw[1:2, :] * h1_ref[...] + cw[0:1, :] * h0_ref[...]
        ys_ref[...] = (bg * conv).astype(ys_ref.dtype)
        us_ref[...] = u

    @pl.when(i % tiles_per_seq == 0)
    def _():
        prev_ref[...] = jnp.zeros_like(prev_ref)

    bg, cg, v = gates(xp_ref[...])
    u = cg * v
    prev8 = prev_ref[...]
    conv = cw[2:3, :] * u + cw[1:2, :] * _shifted_rows(u, prev8, 1) + cw[0:1, :] * _shifted_rows(u, prev8, 2)
    yp_ref[...] = (bg * conv).astype(yp_ref.dtype)
    tm = u.shape[0]
    prev_ref[...] = u[tm - 8:, :]

    @pl.when(i % tiles_per_seq == tiles_per_seq - 1)
    def _():
        hist_ref[...] = u[tm - 2:, :]


def _sc_in_conv(xp, xs, w_in3, layer, w_conv, h0, h1, *, tn=512, tm=1024):
    k = xp.shape[1]
    nb = D_MODEL // tn
    tiles_per_seq = SEQ // tm

    def wspec(part):
        return pl.BlockSpec((None, k, tn), lambda j, i: (layer, 0, j + part * nb))

    sspec = pl.BlockSpec((T_S, tn), lambda j, i: (0, j))
    pspec = pl.BlockSpec((tm, tn), lambda j, i: (i, j))
    return pl.pallas_call(
        functools.partial(_sc_in_kernel, tiles_per_seq=tiles_per_seq),
        grid=(nb, T_P // tm),
        in_specs=[pl.BlockSpec((tm, k), lambda j, i: (i, 0)),
                  pl.BlockSpec((T_S, k), lambda j, i: (0, 0)),
                  wspec(0), wspec(1), wspec(2),
                  pl.BlockSpec((3, tn), lambda j, i: (0, j)), sspec, sspec],
        out_specs=[pspec, sspec, sspec,
                   pl.BlockSpec((None, 2, tn), lambda j, i: (i // tiles_per_seq, 0, j))],
        out_shape=[jax.ShapeDtypeStruct((T_P, D_MODEL), BF16),
                   jax.ShapeDtypeStruct((T_S, D_MODEL), BF16),
                   jax.ShapeDtypeStruct((T_S, D_MODEL), F32),
                   jax.ShapeDtypeStruct((BATCH, 2, D_MODEL), F32)],
        scratch_shapes=[pltpu.VMEM((k, tn), BF16), pltpu.VMEM((k, tn), BF16), pltpu.VMEM((k, tn), BF16),
                        pltpu.VMEM((8, tn), F32)],
        compiler_params=_cparams("arbitrary", "arbitrary"),
        name="sc_in_conv",
    )(xp, xs, w_in3, w_in3, w_in3, w_conv, h0, h1)


def _ssdconv_sample_kernel(x_ref, h0_ref, h1_ref, h2_ref, w_ref, b_ref, o_ref, raw_ref):
    x = x_ref[...]
    w = w_ref[...]
    conv = w[3:4, :] * x + w[2:3, :] * h2_ref[...] + w[1:2, :] * h1_ref[...] + w[0:1, :] * h0_ref[...]
    o_ref[...] = _silu(conv + b_ref[...])
    raw_ref[...] = x


def _ssdconv_sample(zx, h0, h1, h2, conv_w, conv_b, *, tc=512):
    nb = SSD_CONV_DIM // tc
    off = D_INNER // tc
    hspec = pl.BlockSpec((T_S, tc), lambda j: (0, j))
    return pl.pallas_call(
        _ssdconv_sample_kernel,
        grid=(nb,),
        in_specs=[pl.BlockSpec((T_S, tc), lambda j: (0, j + off)), hspec, hspec, hspec,
                  pl.BlockSpec((4, tc), lambda j: (0, j)),
                  pl.BlockSpec((1, tc), lambda j: (0, j))],
        out_specs=[hspec, hspec],
        out_shape=[jax.ShapeDtypeStruct((T_S, SSD_CONV_DIM), F32),
                   jax.ShapeDtypeStruct((T_S, SSD_CONV_DIM), F32)],
        compiler_params=_cparams("arbitrary"),
        name="ssdconv_sample",
    )(zx, h0, h1, h2, conv_w, conv_b)


def _dt_kernel(xp_ref, xs_ref, w_ref, bias_ref, alog_ref, dtp_ref, csp_ref, dts_ref, decs_ref, *, n_p):
    i = pl.program_id(0)

    def dt_da(x_ref):
        w = w_ref[:SSD_HEADS, :].astype(BF16)
        raw = lax.dot_general(x_ref[...], w, _NT_DIMS, preferred_element_type=F32) + bias_ref[...]
        dt = jnp.maximum(raw, 0.0) + jnp.log1p(jnp.exp(-jnp.abs(raw)))
        return dt, dt * (-jnp.exp(alog_ref[...]))

    @pl.when(i < n_p)
    def _():
        dt, da = dt_da(xp_ref)
        dtp_ref[...] = dt
        row_in_chunk = lax.broadcasted_iota(jnp.int32, da.shape, 0) % SSD_CHUNK
        cs = da
        k = 1
        while k < SSD_CHUNK:
            cs = cs + jnp.where(row_in_chunk >= k, pltpu.roll(cs, k, axis=0), 0.0)
            k *= 2
        csp_ref[...] = cs

    @pl.when(i == n_p)
    def _():
        dt, da = dt_da(xs_ref)
        dts_ref[...] = dt
        decs_ref[...] = jnp.exp(da)


def _dt_prep(hn_p, hn_s, w_in3t, layer, dt_bias, a_log):
    k = hn_p.shape[1]
    tm = 8 * SSD_CHUNK
    n_p = T_P // tm
    dt_block = (D_INNER + SSD_CONV_DIM) // LANES
    pin = pl.BlockSpec((tm, k), lambda i: (jnp.minimum(i, n_p - 1), 0))
    pout = pl.BlockSpec((tm, SSD_HEADS), lambda i: (jnp.minimum(i, n_p - 1), 0))
    sout = pl.BlockSpec((T_S, SSD_HEADS), lambda i: (0, 0))
    one = pl.BlockSpec((1, SSD_HEADS), lambda i: (0, 0))
    return pl.pallas_call(
        functools.partial(_dt_kernel, n_p=n_p),
        grid=(n_p + 1,),
        in_specs=[pin, pl.BlockSpec((T_S, k), lambda i: (0, 0)),
                  pl.BlockSpec((None, LANES, k), lambda i: (layer, dt_block, 0)), one, one],
        out_specs=[pout, pout, sout, sout],
        out_shape=[jax.ShapeDtypeStruct((T_P, SSD_HEADS), F32), jax.ShapeDtypeStruct((T_P, SSD_HEADS), F32),
                   jax.ShapeDtypeStruct((T_S, SSD_HEADS), F32), jax.ShapeDtypeStruct((T_S, SSD_HEADS), F32)],
        compiler_params=_cparams("arbitrary"),
        name="ssd_dt",
    )(hn_p, hn_s, w_in3t, dt_bias.reshape(1, SSD_HEADS), a_log.reshape(1, SSD_HEADS))


def _expand_heads(v, onehot):
    hi = v.astype(BF16)
    lo = (v - hi.astype(F32)).astype(BF16)
    return (jnp.dot(hi, onehot, preferred_element_type=F32)
            + jnp.dot(lo, onehot, preferred_element_type=F32))


def _gate_norm(y, z, ng):
    return _rms(y * _silu(z), ng)


def _ssd_group_chunk(x, bm, cm, z, dtc, csc, dtr, csr, dsk, ng, st):
    q = SSD_CHUNK
    xb = x.astype(BF16)
    bb = bm.astype(BF16)
    cb = cm.astype(BF16)

    lane = lax.broadcasted_iota(jnp.int32, (SSD_HPG, GROUP_W), 1)
    head = lax.broadcasted_iota(jnp.int32, (SSD_HPG, GROUP_W), 0)
    onehot = jnp.where(lane // SSD_HEAD_DIM == head, 1.0, 0.0).astype(BF16)

    cbm = lax.dot_general(cb, bb, (((1,), (1,)), ((), ())), preferred_element_type=F32)
    row = lax.broadcasted_iota(jnp.int32, (q, q), 0)
    col = lax.broadcasted_iota(jnp.int32, (q, q), 1)
    tri = row >= col
    lane128 = lax.broadcasted_iota(jnp.int32, (q, LANES), 1)
    lo_half = lane128 < SSD_HEAD_DIM

    tiles = []
    for m in range(GROUP_W // LANES):
        xt = xb[:, LANES * m:LANES * (m + 1)]
        acc = None
        for j, xm in ((2 * m, jnp.where(lo_half, xt, 0)), (2 * m + 1, jnp.where(lo_half, 0, xt))):
            seg = csc[:, j:j + 1] - csr[j:j + 1, :]
            dec = jnp.exp(jnp.where(tri, seg, -jnp.inf))
            w = (cbm * dec * dtr[j:j + 1, :]).astype(BF16)
            part = jnp.dot(w, xm, preferred_element_type=F32)
            acc = part if acc is None else acc + part
        tiles.append(acc)
    y_diag = jnp.concatenate(tiles, axis=1)

    e_exp = _expand_heads(jnp.exp(csc), onehot)
    y_off = jnp.dot(cb, st.astype(BF16), preferred_element_type=F32) * e_exp

    cs_last = csc[q - 1:q, :]
    te = _expand_heads(jnp.exp(cs_last - csc) * dtc, onehot)
    xw = (x * te).astype(BF16)
    d_st = jnp.dot(bm.T.astype(BF16), xw, preferred_element_type=F32)
    st_new = st * e_exp[q - 1:q, :] + d_st
    y = y_diag + y_off + dsk * x
    return _gate_norm(y, z, ng), st_new


def _causal_conv4_silu(x, prev8, w, bias):
    acc = w[3:4, :] * x + bias
    for k in (1, 2, 3):
        acc = acc + w[3 - k:4 - k, :] * _shifted_rows(x, prev8, k)
    return _silu(acc)


def _ssd_scan_kernel(x_ref, b_ref, c_ref, z_ref, wx_ref, wb_ref, wc_ref, bx_ref, bb_ref, bc_ref,
                     dtc_ref, csc_ref, dtr_ref, csr_ref, dsk_ref, ng_ref,
                     y_ref, sfin_ref, hist_ref, st_ref, px_ref, pb_ref, pc_ref):
    q = SSD_CHUNK
    c = pl.program_id(1)

    @pl.when(c == 0)
    def _():
        st_ref[...] = jnp.zeros_like(st_ref)
        px_ref[...] = jnp.zeros_like(px_ref)
        pb_ref[...] = jnp.zeros_like(pb_ref)
        pc_ref[...] = jnp.zeros_like(pc_ref)

    for gi in range(SSD_GROUPS):
        xs = slice(GROUP_W * gi, GROUP_W * (gi + 1))
        ns = slice(SSD_STATE * gi, SSD_STATE * (gi + 1))
        x = _causal_conv4_silu(x_ref[:, xs], px_ref[:, xs], wx_ref[:, xs], bx_ref[:, xs])
        bm = _causal_conv4_silu(b_ref[:, ns], pb_ref[:, ns], wb_ref[:, ns], bb_ref[:, ns])
        cm = _causal_conv4_silu(c_ref[:, ns], pc_ref[:, ns], wc_ref[:, ns], bc_ref[:, ns])
        y, st_new = _ssd_group_chunk(x, bm, cm, z_ref[:, xs],
                                     dtc_ref[gi], csc_ref[gi], dtr_ref[gi], csr_ref[gi],
                                     dsk_ref[gi], ng_ref[gi], st_ref[gi])
        y_ref[:, xs] = y.astype(y_ref.dtype)
        st_ref[gi] = st_new

    px_ref[...] = x_ref[q - 8:, :]
    pb_ref[...] = b_ref[q - 8:, :]
    pc_ref[...] = c_ref[q - 8:, :]

    @pl.when(c == pl.num_programs(1) - 1)
    def _():
        for gi in range(SSD_GROUPS):
            sfin_ref[GROUP_W * gi:GROUP_W * (gi + 1), :] = st_ref[gi].T
        hist_ref[:, 0:D_INNER] = x_ref[q - 3:, :]
        hist_ref[:, D_INNER:D_INNER + BC_W] = b_ref[q - 3:, :]
        hist_ref[:, D_INNER + BC_W:] = c_ref[q - 3:, :]


def _ssd_scan_prompt(zx, conv_w, conv_b, dtc, csc, dtr, csr, dskip, ng):
    q = SSD_CHUNK
    nc = SEQ // q
    b_blk = D_INNER // BC_W
    small_c = pl.BlockSpec((SSD_GROUPS, q, SSD_HPG), lambda b, c: (0, b * nc + c, 0))
    small_r = pl.BlockSpec((SSD_GROUPS, SSD_HPG, q), lambda b, c: (0, 0, b * nc + c))
    pspec = pl.BlockSpec((SSD_GROUPS, 1, GROUP_W), lambda b, c: (0, 0, 0))

    def rows(width, col_block):
        return pl.BlockSpec((q, width), lambda b, c: (b * nc + c, col_block))

    def par(nrows, width, col_block):
        return pl.BlockSpec((nrows, width), lambda b, c: (0, col_block))

    return pl.pallas_call(
        _ssd_scan_kernel,
        grid=(BATCH, nc),
        in_specs=[rows(D_INNER, 1), rows(BC_W, 2 * b_blk), rows(BC_W, 2 * b_blk + 1), rows(D_INNER, 0),
                  par(4, D_INNER, 0), par(4, BC_W, b_blk), par(4, BC_W, b_blk + 1),
                  par(1, D_INNER, 0), par(1, BC_W, b_blk), par(1, BC_W, b_blk + 1),
                  small_c, small_c, small_r, small_r, pspec, pspec],
        out_specs=[rows(D_INNER, 0),
                   pl.BlockSpec((D_INNER, SSD_STATE), lambda b, c: (b, 0)),
                   pl.BlockSpec((None, 3, SSD_CONV_DIM), lambda b, c: (b, 0, 0))],
        out_shape=[jax.ShapeDtypeStruct((T_P, D_INNER), BF16),
                   jax.ShapeDtypeStruct((BATCH * D_INNER, SSD_STATE), F32),
                   jax.ShapeDtypeStruct((BATCH, 3, SSD_CONV_DIM), F32)],
        scratch_shapes=[pltpu.VMEM((SSD_GROUPS, SSD_STATE, GROUP_W), F32),
                        pltpu.VMEM((8, D_INNER), F32), pltpu.VMEM((8, BC_W), F32), pltpu.VMEM((8, BC_W), F32)],
        compiler_params=_cparams("arbitrary", "arbitrary"),
        name="ssd_scan",
    )(zx, zx, zx, zx, conv_w, conv_w, conv_w, conv_b, conv_b, conv_b, dtc, csc, dtr, csr, dskip, ng)


def _ssd_step_kernel(s_ref, xt_ref, dtt_ref, bt_ref, ce_ref, dec_ref, so_ref, yt_ref):
    tile_of_lane = lax.broadcasted_iota(jnp.int32, (N_TILES, D_INNER), 1) // LANES
    row32 = lax.broadcasted_iota(jnp.int32, (N_TILES, D_INNER), 0)
    lane32 = lax.broadcasted_iota(jnp.int32, (SSD_STATE, N_TILES), 1)
    for b in range(SSD_STEP_SEQS):
        xdt = (xt_ref[b] * dtt_ref[b]).astype(BF16)
        b_big = jnp.where(tile_of_lane == row32, bt_ref[b], 0.0).astype(BF16)
        upd = jnp.dot(xdt, b_big, preferred_element_type=F32)

        ce = ce_ref[b]
        s_tiles, c_tiles = [], []
        for r in range(N_TILES):
            dtile = jnp.concatenate(
                [jnp.broadcast_to(dec_ref[b, 2 * r:2 * r + 1, :], (SSD_HEAD_DIM, SSD_STATE)),
                 jnp.broadcast_to(dec_ref[b, 2 * r + 1:2 * r + 2, :], (SSD_HEAD_DIM, SSD_STATE))], axis=0)
            s_n = s_ref[b, LANES * r:LANES * (r + 1), :] * dtile + upd[:, LANES * r:LANES * (r + 1)]
            so_ref[b, LANES * r:LANES * (r + 1), :] = s_n
            s_tiles.append(s_n.astype(BF16))
            c_tiles.append(jnp.where(lane32 == r, ce, 0.0).astype(BF16))
        s_big = jnp.concatenate(s_tiles, axis=1)
        c_big = jnp.concatenate(c_tiles, axis=0)
        yt_ref[b] = jnp.dot(s_big, c_big, preferred_element_type=F32)


def _ssd_step_sample(state, x_t, dt_t, b_tiled, c_exp, dec):
    nb = SSD_STEP_SEQS
    tspec = pl.BlockSpec((nb, LANES, N_TILES), lambda b: (b, 0, 0))
    sspec = pl.BlockSpec((nb, D_INNER, SSD_STATE), lambda b: (b, 0, 0))
    return pl.pallas_call(
        _ssd_step_kernel,
        grid=(DEC_BATCH // nb,),
        in_specs=[sspec, tspec, tspec,
                  pl.BlockSpec((nb, 1, D_INNER), lambda b: (b, 0, 0)),
                  pl.BlockSpec((nb, SSD_STATE, N_TILES), lambda b: (b, 0, 0)),
                  pl.BlockSpec((nb, SSD_HEADS, SSD_STATE), lambda b: (b, 0, 0))],
        out_specs=[sspec, tspec],
        out_shape=[jax.ShapeDtypeStruct((DEC_BATCH, D_INNER, SSD_STATE), F32),
                   jax.ShapeDtypeStruct((DEC_BATCH, LANES, N_TILES), F32)],
        compiler_params=_cparams("arbitrary"),
        name="ssd_step",
    )(state, x_t, dt_t, b_tiled, c_exp, dec)


def _gate_norm_kernel(y_ref, x_ref, z_ref, dsk_ref, ng_ref, o_ref):
    y = y_ref[...] + dsk_ref[...] * x_ref[...]
    o_ref[...] = _gate_norm(y, z_ref[...], ng_ref[...]).astype(o_ref.dtype)


def _gate_norm_sample(y_s, xbc_c_s, zx_s, dskip, ng):
    gspec = pl.BlockSpec((T_S, GROUP_W), lambda g: (0, g))
    pspec = pl.BlockSpec((None, 1, GROUP_W), lambda g: (g, 0, 0))
    return pl.pallas_call(
        _gate_norm_kernel,
        grid=(SSD_GROUPS,),
        in_specs=[gspec, gspec, gspec, pspec, pspec],
        out_specs=gspec,
        out_shape=jax.ShapeDtypeStruct((T_S, D_INNER), BF16),
        compiler_params=_cparams("arbitrary"),
        name="ssd_gate_norm",
    )(y_s, xbc_c_s, zx_s, dskip, ng)


def _xattn_prompt_kernel(q_ref, k_ref, v_ref, o_ref):
    scale = MEM_HEAD_DIM ** -0.5
    for h in range(MEM_HEADS):
        sl = slice(MEM_HEAD_DIM * h, MEM_HEAD_DIM * (h + 1))
        kh = k_ref[:, sl].astype(BF16)
        vh = v_ref[:, sl].astype(BF16)
        s = lax.dot_general(q_ref[:, sl], kh, (((1,), (1,)), ((), ())), preferred_element_type=F32) * scale
        m = jnp.max(s, axis=-1, keepdims=True)
        e = jnp.exp(s - m)
        p = (e / jnp.sum(e, axis=-1, keepdims=True)).astype(BF16)
        o_ref[:, sl] = jnp.dot(p, vh, preferred_element_type=F32).astype(o_ref.dtype)


def _xattn_prompt(q, k, v, *, tq=512):
    nq = SEQ // tq
    kspec = pl.BlockSpec((MEM_LEN, D_MODEL), lambda b, i: (b, 0))
    return pl.pallas_call(
        _xattn_prompt_kernel,
        grid=(BATCH, nq),
        in_specs=[pl.BlockSpec((tq, D_MODEL), lambda b, i: (b * nq + i, 0)), kspec, kspec],
        out_specs=pl.BlockSpec((tq, D_MODEL), lambda b, i: (b * nq + i, 0)),
        out_shape=jax.ShapeDtypeStruct((T_P, D_MODEL), BF16),
        compiler_params=_cparams("arbitrary", "arbitrary"),
        name="xattn_prompt",
    )(q, k, v)


def _xattn_sample_kernel(q_ref, k_ref, v_ref, o_ref, *, bb):
    scale = MEM_HEAD_DIM ** -0.5
    for b in range(bb):
        for h in range(MEM_HEADS):
            kh = k_ref[b, :, h, :]
            s = jnp.sum(kh * q_ref[b, h:h + 1, :], axis=-1, keepdims=True) * scale
            m = jnp.max(s, axis=0, keepdims=True)
            e = jnp.exp(s - m)
            p = e / jnp.sum(e, axis=0, keepdims=True)
            o_ref[b, h:h + 1, :] = jnp.sum(p * v_ref[b, :, h, :], axis=0, keepdims=True)


def _xattn_sample(q_s, kc, vc, layer, *, bb=4):
    cspec = pl.BlockSpec((None, bb, MEM_LEN, MEM_HEADS, MEM_HEAD_DIM), lambda i: (layer, i, 0, 0, 0))
    qspec = pl.BlockSpec((bb, MEM_HEADS, MEM_HEAD_DIM), lambda i: (i, 0, 0))
    return pl.pallas_call(
        functools.partial(_xattn_sample_kernel, bb=bb),
        grid=(DEC_BATCH // bb,),
        in_specs=[qspec, cspec, cspec],
        out_specs=qspec,
        out_shape=jax.ShapeDtypeStruct((DEC_BATCH, MEM_HEADS, MEM_HEAD_DIM), F32),
        compiler_params=_cparams("arbitrary"),
        name="xattn_sample",
    )(q_s, kc, vc)


def _mem_attn_block(hp, hs, i, norm_mem_q, xa_w_q, xa_w_o, k_p, v_p, kc, vc):
    hq_p, hq_s = _rmsnorm2(hp, hs, norm_mem_q[i], BF16)
    q_p, q_s = _matmul2(hq_p, hq_s, [xa_w_q], i, tn=1024, tm=1024, out_dtype=BF16)
    o_p = _xattn_prompt(q_p, k_p, v_p)
    o_s = _xattn_sample(q_s.astype(F32).reshape(DEC_BATCH, MEM_HEADS, MEM_HEAD_DIM), kc, vc, i)
    o_s = o_s.reshape(DEC_BATCH, D_MODEL).astype(BF16)
    return _matmul2(o_p, o_s, [xa_w_o], i, tn=1024, tm=1024, res=(hp, hs))


def _ffn_block(hp, hs, i, norm_ffn, w_gate, w_up, w_down):
    hf_p, hf_s = _rmsnorm2(hp, hs, norm_ffn[i], BF16)
    a_p, a_s = _matmul2(hf_p, hf_s, [w_gate, w_up], i, tn=512, tm=1024, out_dtype=BF16, epilogue=_swiglu_epilogue)
    return _matmul2(a_p, a_s, [w_down], i, tn=512, tm=512, res=(hp, hs))


def _short_conv_block(hp, hs, a, norm_g, cache_sc, sc_w_in, sc_w_conv, sc_w_out):
    hn_p, hn_s = _rmsnorm2(hp, hs, norm_g, BF16)
    h0, h1 = cache_sc[a, :, 0, :], cache_sc[a, :, 1, :]
    y_p, y_s, u_s, hist_p = _sc_in_conv(hn_p, hn_s, sc_w_in, a, sc_w_conv[a], h0, h1)
    hp, hs = _matmul2(y_p, y_s, [sc_w_out], a, tn=1024, tm=1024, res=(hp, hs))
    return hp, hs, hist_p, jnp.stack([h1, u_s], axis=1)


def _ssd_block(hp, hs, j, norm_g, conv_hist, state, ssd_w_in, conv_w, conv_b, dt_bias, a_log, d_skip, norm_ssd, w_out):
    hn_p, hn_s = _rmsnorm2(hp, hs, norm_g, BF16)
    w_in_t = jnp.swapaxes(ssd_w_in, 1, 2)
    zx_p, zx_s = _matmul2(hn_p, hn_s, [w_in_t], j, tn=1024, tm=1024, n_out=D_INNER + SSD_CONV_DIM,
                          w_transposed=True)

    dt_p, cs_p, dt_s, dec_s = _dt_prep(hn_p, hn_s, w_in_t, j, dt_bias[j], a_log[j])

    cw, cbias = conv_w[j], conv_b[j].reshape(1, SSD_CONV_DIM)
    ch = conv_hist[j]
    xbc_c_s, raw_s = _ssdconv_sample(zx_s, ch[:, 0, :], ch[:, 1, :], ch[:, 2, :], cw, cbias)
    chist_s = jnp.stack([ch[:, 1, :], ch[:, 2, :], raw_s], axis=1)

    dskip = jnp.repeat(d_skip[j], SSD_HEAD_DIM).reshape(SSD_GROUPS, 1, GROUP_W)
    ng = norm_ssd[j].reshape(SSD_GROUPS, 1, GROUP_W)

    def per_group(v):
        v = v[:, :SSD_HEADS].reshape(T_P, SSD_GROUPS, SSD_HPG)
        return jnp.transpose(v, (1, 0, 2)), jnp.transpose(v, (1, 2, 0))

    dtc, dtr = per_group(dt_p)
    csc, csr = per_group(cs_p)
    g_p, sfin, chist_p = _ssd_scan_prompt(zx_p, cw, cbias, dtc, csc, dtr, csr, dskip, ng)
    ss_p = sfin.reshape(BATCH, SSD_HEADS, SSD_HEAD_DIM, SSD_STATE)

    def tile_major(v):
        return jnp.transpose(v.reshape(DEC_BATCH, N_TILES, LANES), (0, 2, 1))

    tiles_per_group = GROUP_W // LANES
    x_t = tile_major(xbc_c_s[:, :D_INNER])
    dt_t = tile_major(jnp.repeat(dt_s[:, :SSD_HEADS], SSD_HEAD_DIM, axis=1))
    b_g = xbc_c_s[:, D_INNER:D_INNER + BC_W].reshape(DEC_BATCH, SSD_GROUPS, SSD_STATE)
    c_g = xbc_c_s[:, D_INNER + BC_W:].reshape(DEC_BATCH, SSD_GROUPS, SSD_STATE)
    b_tiled = jnp.repeat(b_g, tiles_per_group, axis=1).reshape(DEC_BATCH, 1, D_INNER)
    c_exp = jnp.transpose(jnp.repeat(c_g, tiles_per_group, axis=1), (0, 2, 1))
    dec_b = jnp.broadcast_to(dec_s[:, :SSD_HEADS, None], (DEC_BATCH, SSD_HEADS, SSD_STATE))
    s_new, y_t = _ssd_step_sample(state[j].reshape(DEC_BATCH, D_INNER, SSD_STATE), x_t, dt_t, b_tiled, c_exp, dec_b)
    y_s = jnp.transpose(y_t, (0, 2, 1)).reshape(DEC_BATCH, D_INNER)
    g_s = _gate_norm_sample(y_s, xbc_c_s, zx_s, dskip, ng)
    ss_s = s_new.reshape(DEC_BATCH, SSD_HEADS, SSD_HEAD_DIM, SSD_STATE)

    hp, hs = _matmul2(g_p, g_s, [w_out], j, tn=512, tm=1024, res=(hp, hs))
    return hp, hs, chist_p, chist_s, ss_p, ss_s


def kernel(x_prompt, x_sample, mem_prompt, cache_sc, state_ssd_conv, state_ssd, cache_mem_k, cache_mem_v, norm_mix, norm_mem_q, norm_mem_kv, norm_ffn, norm_final, sc_w_in, sc_w_conv, sc_w_out, ssd_w_in, ssd_conv_w, ssd_conv_b, ssd_dt_bias, ssd_a_log, ssd_d, ssd_norm, ssd_w_out, xa_w_q, xa_w_k, xa_w_v, xa_w_o, ffn_w_gate, ffn_w_up, ffn_w_down):
    depth = norm_mix.shape[0]
    hp = x_prompt.reshape(T_P, D_MODEL)
    hs = x_sample.reshape(T_S, D_MODEL)

    mem = mem_prompt.reshape(BATCH * MEM_LEN, D_MODEL)
    k_ps, v_ps = [], []
    for i in range(depth):
        m = _rmsnorm1(mem, norm_mem_kv[i], BF16, tm=512)
        k_ps.append(_matmul2(m, None, [xa_w_k], i, tn=1024, tm=512))
        v_ps.append(_matmul2(m, None, [xa_w_v], i, tn=1024, tm=512))

    sc_p, sc_s, sconv_p, sconv_s, ss_p, ss_s = [], [], [], [], [], []
    for i in range(depth):
        if i % 2 == 0:
            hp, hs, a_p, a_s = _short_conv_block(hp, hs, i // 2, norm_mix[i], cache_sc, sc_w_in, sc_w_conv, sc_w_out)
            sc_p.append(a_p)
            sc_s.append(a_s)
        else:
            hp, hs, c_p, c_s, s_p, s_s = _ssd_block(hp, hs, i // 2, norm_mix[i], state_ssd_conv, state_ssd, ssd_w_in,
                                                    ssd_conv_w, ssd_conv_b, ssd_dt_bias, ssd_a_log, ssd_d, ssd_norm,
                                                    ssd_w_out)
            sconv_p.append(c_p)
            sconv_s.append(c_s)
            ss_p.append(s_p)
            ss_s.append(s_s)
        hp, hs = _mem_attn_block(hp, hs, i, norm_mem_q, xa_w_q, xa_w_o, k_ps[i], v_ps[i], cache_mem_k, cache_mem_v)
        hp, hs = _ffn_block(hp, hs, i, norm_ffn, ffn_w_gate, ffn_w_up, ffn_w_down)

    y_p, y_s = _rmsnorm2(hp, hs, norm_final, F32)

    kv_shape = (depth, BATCH, MEM_LEN, MEM_HEADS, MEM_HEAD_DIM)
    return (y_p.reshape(BATCH, SEQ, D_MODEL), y_s.reshape(DEC_BATCH, 1, D_MODEL),
            jnp.stack(sc_p), jnp.stack(sc_s), jnp.stack(sconv_p), jnp.stack(sconv_s),
            jnp.stack(ss_p), jnp.stack(ss_s),
            jnp.stack(k_ps).reshape(kv_shape), jnp.stack(v_ps).reshape(kv_shape))
```

```python
import functools

import jax
import jax.numpy as jnp
from jax import lax
from jax.experimental import pallas as pl
from jax.experimental.pallas import tpu as pltpu

F32 = jnp.float32
BF16 = jnp.bfloat16

D_MODEL = 2048
BATCH = 4
SEQ = 2048
DEC_BATCH = 128
T_P = BATCH * SEQ
T_S = DEC_BATCH
D_INNER = 4096
SSD_HEAD_DIM = 64
SSD_HEADS = 64
SSD_GROUPS = 8
SSD_HPG = 8
SSD_STATE = 128
GROUP_W = SSD_HPG * SSD_HEAD_DIM
BC_W = SSD_GROUPS * SSD_STATE
SSD_CONV_DIM = D_INNER + 2 * BC_W
MEM_LEN = 256
MEM_HEADS = 4
MEM_HEAD_DIM = 512
RMS_EPS = 1e-5
LANES = 128

V7X_VMEM_LIMIT_BYTES = 56 * 1024 * 1024
SSD_CHUNK = 128
SSD_STEP_SEQS = 2
N_TILES = D_INNER // LANES


def _cparams(*sem):
    return pltpu.CompilerParams(dimension_semantics=sem, vmem_limit_bytes=V7X_VMEM_LIMIT_BYTES)


def _silu(x):
    return x * jax.nn.sigmoid(x)


def _rms(x, g):
    ms = jnp.mean(x * x, axis=-1, keepdims=True)
    return x * lax.rsqrt(ms + RMS_EPS) * g


def _rmsnorm_kernel(xp_ref, xs_ref, g_ref, op_ref, os_ref, *, n_p):
    i = pl.program_id(0)

    @pl.when(i < n_p)
    def _():
        op_ref[...] = _rms(xp_ref[...], g_ref[...]).astype(op_ref.dtype)

    @pl.when(i == n_p)
    def _():
        os_ref[...] = _rms(xs_ref[...], g_ref[...]).astype(os_ref.dtype)


def _rmsnorm2(xp, xs, g, out_dtype, *, tm=1024):
    d = xp.shape[1]
    n_p = xp.shape[0] // tm
    pspec = pl.BlockSpec((tm, d), lambda i: (jnp.minimum(i, n_p - 1), 0))
    sspec = pl.BlockSpec((T_S, d), lambda i: (0, 0))
    return pl.pallas_call(
        functools.partial(_rmsnorm_kernel, n_p=n_p),
        grid=(n_p + 1,),
        in_specs=[pspec, sspec, pl.BlockSpec((1, d), lambda i: (0, 0))],
        out_specs=[pspec, sspec],
        out_shape=[jax.ShapeDtypeStruct(xp.shape, out_dtype), jax.ShapeDtypeStruct(xs.shape, out_dtype)],
        compiler_params=_cparams("arbitrary"),
        name="rmsnorm",
    )(xp, xs, g.reshape(1, d))


def _rmsnorm1_kernel(x_ref, g_ref, o_ref):
    o_ref[...] = _rms(x_ref[...], g_ref[...]).astype(o_ref.dtype)


def _rmsnorm1(x, g, out_dtype, *, tm):
    rows, d = x.shape
    return pl.pallas_call(
        _rmsnorm1_kernel,
        grid=(rows // tm,),
        in_specs=[pl.BlockSpec((tm, d), lambda i: (i, 0)), pl.BlockSpec((1, d), lambda i: (0, 0))],
        out_specs=pl.BlockSpec((tm, d), lambda i: (i, 0)),
        out_shape=jax.ShapeDtypeStruct((rows, d), out_dtype),
        compiler_params=_cparams("arbitrary"),
        name="rmsnorm_mem",
    )(x, g.reshape(1, d))


_NT_DIMS = (((1,), (1,)), ((), ()))


def _mm_kernel(*refs, has_s, has_res, n_w, epilogue, w_transposed):
    it = iter(refs)
    xp_ref = next(it)
    xs_ref = next(it) if has_s else None
    w_refs = [next(it) for _ in range(n_w)]
    rp_ref = next(it) if has_res else None
    rs_ref = next(it) if (has_res and has_s) else None
    op_ref = next(it)
    os_ref = next(it) if has_s else None
    wb_refs = [next(it) for _ in range(n_w)]

    def run(x_ref, r_ref, o_ref):
        x = x_ref[...]
        if w_transposed:
            accs = [lax.dot_general(x, wb_ref[...], _NT_DIMS, preferred_element_type=F32) for wb_ref in wb_refs]
        else:
            accs = [jnp.dot(x, wb_ref[...], preferred_element_type=F32) for wb_ref in wb_refs]
        out = epilogue(*accs)
        if r_ref is not None:
            out = out + r_ref[...]
        o_ref[...] = out.astype(o_ref.dtype)

    @pl.when(pl.program_id(1) == 0)
    def _():
        for w_ref, wb_ref in zip(w_refs, wb_refs):
            wb_ref[...] = w_ref[...].astype(BF16)
        if has_s:
            run(xs_ref, rs_ref, os_ref)

    run(xp_ref, rp_ref, op_ref)


def _identity(a):
    return a


def _swiglu_epilogue(a, u):
    return _silu(a) * u


def _matmul2(xp, xs, ws, layer, *, tn, tm, n_out=None, res=None, out_dtype=F32, epilogue=_identity,
             w_transposed=False):
    m, k = xp.shape
    n = n_out if n_out is not None else ws[0].shape[1 if w_transposed else 2]
    has_s = xs is not None
    xp_spec = pl.BlockSpec((tm, k), lambda j, i: (i, 0))
    xs_spec = pl.BlockSpec((T_S, k), lambda j, i: (0, 0))
    if w_transposed:
        w_spec = pl.BlockSpec((None, tn, k), lambda j, i: (layer, j, 0))
    else:
        w_spec = pl.BlockSpec((None, k, tn), lambda j, i: (layer, 0, j))
    op_spec = pl.BlockSpec((tm, tn), lambda j, i: (i, j))
    os_spec = pl.BlockSpec((T_S, tn), lambda j, i: (0, j))

    in_specs, args = [xp_spec], [xp]
    if has_s:
        in_specs.append(xs_spec)
        args.append(xs)
    in_specs += [w_spec] * len(ws)
    args += list(ws)
    if res is not None:
        in_specs.append(op_spec)
        args.append(res[0])
        if has_s:
            in_specs.append(os_spec)
            args.append(res[1])
    out_specs = [op_spec] + ([os_spec] if has_s else [])
    out_shape = [jax.ShapeDtypeStruct((m, n), out_dtype)] + ([jax.ShapeDtypeStruct((T_S, n), out_dtype)] if has_s else [])
    outs = pl.pallas_call(
        functools.partial(_mm_kernel, has_s=has_s, has_res=res is not None, n_w=len(ws), epilogue=epilogue,
                          w_transposed=w_transposed),
        grid=(n // tn, m // tm),
        in_specs=in_specs,
        out_specs=out_specs,
        out_shape=out_shape,
        scratch_shapes=[pltpu.VMEM((tn, k) if w_transposed else (k, tn), BF16) for _ in ws],
        compiler_params=_cparams("arbitrary", "arbitrary"),
        name="matmul",
    )(*args)
    return tuple(outs) if has_s else outs[0]


def _mm_res_norm_kernel(xp_ref, xs_ref, w_ref, rp_ref, rs_ref, g_ref, hp_ref, hs_ref, np_ref, ns_ref, wb_ref):
    def run(x_ref, r_ref, h_ref, n_ref):
        h = jnp.dot(x_ref[...], wb_ref[...], preferred_element_type=F32) + r_ref[...]
        h_ref[...] = h
        n_ref[...] = _rms(h, g_ref[...]).astype(n_ref.dtype)

    @pl.when(pl.program_id(0) == 0)
    def _():
        wb_ref[...] = w_ref[...].astype(BF16)
        run(xs_ref, rs_ref, hs_ref, ns_ref)

    run(xp_ref, rp_ref, hp_ref, np_ref)


def _matmul_res_norm(xp, xs, w3, layer, res, g, *, tm=512):
    m, k = xp.shape
    n = w3.shape[2]
    pspec = pl.BlockSpec((tm, n), lambda i: (i, 0))
    sspec = pl.BlockSpec((T_S, n), lambda i: (0, 0))
    wspec = pl.BlockSpec((None, k, n), lambda i: (layer, 0, 0), pipeline_mode=pl.Buffered(1))
    return pl.pallas_call(
        _mm_res_norm_kernel,
        grid=(m // tm,),
        in_specs=[pl.BlockSpec((tm, k), lambda i: (i, 0)), pl.BlockSpec((T_S, k), lambda i: (0, 0)),
                  wspec, pspec, sspec, pl.BlockSpec((1, n), lambda i: (0, 0))],
        out_specs=[pspec, sspec, pspec, sspec],
        out_shape=[jax.ShapeDtypeStruct((m, n), F32), jax.ShapeDtypeStruct((T_S, n), F32),
                   jax.ShapeDtypeStruct((m, n), BF16), jax.ShapeDtypeStruct((T_S, n), BF16)],
        scratch_shapes=[pltpu.VMEM((k, n), BF16)],
        compiler_params=_cparams("arbitrary"),
        name="matmul_res_norm",
    )(xp, xs, w3, res[0], res[1], g.reshape(1, n))


def _shifted_rows(u, prev8, k):
    row8 = lax.broadcasted_iota(jnp.int32, prev8.shape, 0)
    r = pltpu.roll(u, k, axis=0)
    top = jnp.where(row8 < k, pltpu.roll(prev8, k, axis=0), r[0:8, :])
    return jnp.concatenate([top, r[8:, :]], axis=0)


def _sc_in_kernel(xp_ref, xs_ref, wb_ref, wc_ref, wv_ref, cw_ref, h0_ref, h1_ref,
                  yp_ref, ys_ref, us_ref, hist_ref, wbb_ref, wcb_ref, wvb_ref, prev_ref, *, tiles_per_seq):
    i = pl.program_id(1)
    cw = cw_ref[...]

    def gates(x):
        return (jnp.dot(x, wbb_ref[...], preferred_element_type=F32),
                jnp.dot(x, wcb_ref[...], preferred_element_type=F32),
                jnp.dot(x, wvb_ref[...], preferred_element_type=F32))

    @pl.when(i == 0)
    def _():
        wbb_ref[...] = wb_ref[...].astype(BF16)
        wcb_ref[...] = wc_ref[...].astype(BF16)
        wvb_ref[...] = wv_ref[...].astype(BF16)
        bg, cg, v = gates(xs_ref[...])
        u = cg * v
        conv = cw[2:3, :] * u + cw[1:2, :] * h1_ref[...] + cw[0:1, :] * h0_ref[...]
        ys_ref[...] = (bg * conv).astype(ys_ref.dtype)
        us_ref[...] = u

    @pl.when(i % tiles_per_seq == 0)
    def _():
        prev_ref[...] = jnp.zeros_like(prev_ref)

    bg, cg, v = gates(xp_ref[...])
    u = cg * v
    prev8 = prev_ref[...]
    conv = cw[2:3, :] * u + cw[1:2, :] * _shifted_rows(u, prev8, 1) + cw[0:1, :] * _shifted_rows(u, prev8, 2)
    yp_ref[...] = (bg * conv).astype(yp_ref.dtype)
    tm = u.shape[0]
    prev_ref[...] = u[tm - 8:, :]

    @pl.when(i % tiles_per_seq == tiles_per_seq - 1)
    def _():
        hist_ref[...] = u[tm - 2:, :]


def _sc_in_conv(xp, xs, w_in3, layer, w_conv, h0, h1, *, tn=512, tm=1024):
    k = xp.shape[1]
    nb = D_MODEL // tn
    tiles_per_seq = SEQ // tm

    def wspec(part):
        return pl.BlockSpec((None, k, tn), lambda j, i: (layer, 0, j + part * nb))

    sspec = pl.BlockSpec((T_S, tn), lambda j, i: (0, j))
    pspec = pl.BlockSpec((tm, tn), lambda j, i: (i, j))
    return pl.pallas_call(
        functools.partial(_sc_in_kernel, tiles_per_seq=tiles_per_seq),
        grid=(nb, T_P // tm),
        in_specs=[pl.BlockSpec((tm, k), lambda j, i: (i, 0)),
                  pl.BlockSpec((T_S, k), lambda j, i: (0, 0)),
                  wspec(0), wspec(1), wspec(2),
                  pl.BlockSpec((3, tn), lambda j, i: (0, j)), sspec, sspec],
        out_specs=[pspec, sspec, sspec,
                   pl.BlockSpec((None, 2, tn), lambda j, i: (i // tiles_per_seq, 0, j))],
        out_shape=[jax.ShapeDtypeStruct((T_P, D_MODEL), BF16),
                   jax.ShapeDtypeStruct((T_S, D_MODEL), BF16),
                   jax.ShapeDtypeStruct((T_S, D_MODEL), F32),
                   jax.ShapeDtypeStruct((BATCH, 2, D_MODEL), F32)],
        scratch_shapes=[pltpu.VMEM((k, tn), BF16), pltpu.VMEM((k, tn), BF16), pltpu.VMEM((k, tn), BF16),
                        pltpu.VMEM((8, tn), F32)],
        compiler_params=_cparams("arbitrary", "arbitrary"),
        name="sc_in_conv",
    )(xp, xs, w_in3, w_in3, w_in3, w_conv, h0, h1)


def _ssdconv_sample_kernel(x_ref, h0_ref, h1_ref, h2_ref, w_ref, b_ref, o_ref, raw_ref):
    x = x_ref[...]
    w = w_ref[...]
    conv = w[3:4, :] * x + w[2:3, :] * h2_ref[...] + w[1:2, :] * h1_ref[...] + w[0:1, :] * h0_ref[...]
    o_ref[...] = _silu(conv + b_ref[...])
    raw_ref[...] = x


def _ssdconv_sample(zx, h0, h1, h2, conv_w, conv_b, *, tc=512):
    nb = SSD_CONV_DIM // tc
    off = D_INNER // tc
    hspec = pl.BlockSpec((T_S, tc), lambda j: (0, j))
    return pl.pallas_call(
        _ssdconv_sample_kernel,
        grid=(nb,),
        in_specs=[pl.BlockSpec((T_S, tc), lambda j: (0, j + off)), hspec, hspec, hspec,
                  pl.BlockSpec((4, tc), lambda j: (0, j)),
                  pl.BlockSpec((1, tc), lambda j: (0, j))],
        out_specs=[hspec, hspec],
        out_shape=[jax.ShapeDtypeStruct((T_S, SSD_CONV_DIM), F32),
                   jax.ShapeDtypeStruct((T_S, SSD_CONV_DIM), F32)],
        compiler_params=_cparams("arbitrary"),
        name="ssdconv_sample",
    )(zx, h0, h1, h2, conv_w, conv_b)


def _dt_kernel(xp_ref, xs_ref, w_ref, bias_ref, alog_ref, dtp_ref, csp_ref, dts_ref, decs_ref, *, n_p):
    i = pl.program_id(0)

    def dt_da(x_ref):
        w = w_ref[:SSD_HEADS, :].astype(BF16)
        raw = lax.dot_general(x_ref[...], w, _NT_DIMS, preferred_element_type=F32) + bias_ref[...]
        dt = jnp.maximum(raw, 0.0) + jnp.log1p(jnp.exp(-jnp.abs(raw)))
        return dt, dt * (-jnp.exp(alog_ref[...]))

    @pl.when(i < n_p)
    def _():
        dt, da = dt_da(xp_ref)
        dtp_ref[...] = dt
        row_in_chunk = lax.broadcasted_iota(jnp.int32, da.shape, 0) % SSD_CHUNK
        cs = da
        k = 1
        while k < SSD_CHUNK:
            cs = cs + jnp.where(row_in_chunk >= k, pltpu.roll(cs, k, axis=0), 0.0)
            k *= 2
        csp_ref[...] = cs

    @pl.when(i == n_p)
    def _():
        dt, da = dt_da(xs_ref)
        dts_ref[...] = dt
        decs_ref[...] = jnp.exp(da)


def _dt_prep(hn_p, hn_s, w_in3t, layer, dt_bias, a_log):
    k = hn_p.shape[1]
    tm = 8 * SSD_CHUNK
    n_p = T_P // tm
    dt_block = (D_INNER + SSD_CONV_DIM) // LANES
    pin = pl.BlockSpec((tm, k), lambda i: (jnp.minimum(i, n_p - 1), 0))
    pout = pl.BlockSpec((tm, SSD_HEADS), lambda i: (jnp.minimum(i, n_p - 1), 0))
    sout = pl.BlockSpec((T_S, SSD_HEADS), lambda i: (0, 0))
    one = pl.BlockSpec((1, SSD_HEADS), lambda i: (0, 0))
    return pl.pallas_call(
        functools.partial(_dt_kernel, n_p=n_p),
        grid=(n_p + 1,),
        in_specs=[pin, pl.BlockSpec((T_S, k), lambda i: (0, 0)),
                  pl.BlockSpec((None, LANES, k), lambda i: (layer, dt_block, 0)), one, one],
        out_specs=[pout, pout, sout, sout],
        out_shape=[jax.ShapeDtypeStruct((T_P, SSD_HEADS), F32), jax.ShapeDtypeStruct((T_P, SSD_HEADS), F32),
                   jax.ShapeDtypeStruct((T_S, SSD_HEADS), F32), jax.ShapeDtypeStruct((T_S, SSD_HEADS), F32)],
        compiler_params=_cparams("arbitrary"),
        name="ssd_dt",
    )(hn_p, hn_s, w_in3t, dt_bias.reshape(1, SSD_HEADS), a_log.reshape(1, SSD_HEADS))


def _expand_heads(v, onehot):
    hi = v.astype(BF16)
    lo = (v - hi.astype(F32)).astype(BF16)
    return (jnp.dot(hi, onehot, preferred_element_type=F32)
            + jnp.dot(lo, onehot, preferred_element_type=F32))


def _gate_norm(y, z, ng):
    return _rms(y * _silu(z), ng)


def _ssd_group_chunk(x, bm, cm, z, dtc, csc, dtr, csr, dsk, ng, st):
    q = SSD_CHUNK
    xb = x.astype(BF16)
    bb = bm.astype(BF16)
    cb = cm.astype(BF16)

    lane = lax.broadcasted_iota(jnp.int32, (SSD_HPG, GROUP_W), 1)
    head = lax.broadcasted_iota(jnp.int32, (SSD_HPG, GROUP_W), 0)
    onehot = jnp.where(lane // SSD_HEAD_DIM == head, 1.0, 0.0).astype(BF16)

    cbm = lax.dot_general(cb, bb, (((1,), (1,)), ((), ())), preferred_element_type=F32)
    row = lax.broadcasted_iota(jnp.int32, (q, q), 0)
    col = lax.broadcasted_iota(jnp.int32, (q, q), 1)
    tri = row >= col
    lane128 = lax.broadcasted_iota(jnp.int32, (q, LANES), 1)
    lo_half = lane128 < SSD_HEAD_DIM

    tiles = []
    for m in range(GROUP_W // LANES):
        xt = xb[:, LANES * m:LANES * (m + 1)]
        acc = None
        for j, xm in ((2 * m, jnp.where(lo_half, xt, 0)), (2 * m + 1, jnp.where(lo_half, 0, xt))):
            seg = csc[:, j:j + 1] - csr[j:j + 1, :]
            dec = jnp.exp(jnp.where(tri, seg, -jnp.inf))
            w = (cbm * dec * dtr[j:j + 1, :]).astype(BF16)
            part = jnp.dot(w, xm, preferred_element_type=F32)
            acc = part if acc is None else acc + part
        tiles.append(acc)
    y_diag = jnp.concatenate(tiles, axis=1)

    e_exp = _expand_heads(jnp.exp(csc), onehot)
    y_off = jnp.dot(cb, st.astype(BF16), preferred_element_type=F32) * e_exp

    cs_last = csc[q - 1:q, :]
    te = _expand_heads(jnp.exp(cs_last - csc) * dtc, onehot)
    xw = (x * te).astype(BF16)
    d_st = jnp.dot(bm.T.astype(BF16), xw, preferred_element_type=F32)
    st_new = st * e_exp[q - 1:q, :] + d_st
    y = y_diag + y_off + dsk * x
    return _gate_norm(y, z, ng), st_new


def _causal_conv4_silu(x, prev8, w, bias):
    acc = w[3:4, :] * x + bias
    for k in (1, 2, 3):
        acc = acc + w[3 - k:4 - k, :] * _shifted_rows(x, prev8, k)
    return _silu(acc)


def _ssd_scan_kernel(x_ref, b_ref, c_ref, z_ref, wx_ref, wb_ref, wc_ref, bx_ref, bb_ref, bc_ref,
                     dtc_ref, csc_ref, dtr_ref, csr_ref, dsk_ref, ng_ref,
                     y_ref, sfin_ref, hist_ref, st_ref, px_ref, pb_ref, pc_ref):
    q = SSD_CHUNK
    c = pl.program_id(1)

    @pl.when(c == 0)
    def _():
        st_ref[...] = jnp.zeros_like(st_ref)
        px_ref[...] = jnp.zeros_like(px_ref)
        pb_ref[...] = jnp.zeros_like(pb_ref)
        pc_ref[...] = jnp.zeros_like(pc_ref)

    for gi in range(SSD_GROUPS):
        xs = slice(GROUP_W * gi, GROUP_W * (gi + 1))
        ns = slice(SSD_STATE * gi, SSD_STATE * (gi + 1))
        hs = slice(SSD_HPG * gi, SSD_HPG * (gi + 1))
        x = _causal_conv4_silu(x_ref[:, xs], px_ref[:, xs], wx_ref[:, xs], bx_ref[:, xs])
        bm = _causal_conv4_silu(b_ref[:, ns], pb_ref[:, ns], wb_ref[:, ns], bb_ref[:, ns])
        cm = _causal_conv4_silu(c_ref[:, ns], pc_ref[:, ns], wc_ref[:, ns], bc_ref[:, ns])
        y, st_new = _ssd_group_chunk(x, bm, cm, z_ref[:, xs],
                                     dtc_ref[:, hs], csc_ref[:, hs], dtr_ref[hs, :], csr_ref[hs, :],
                                     dsk_ref[gi], ng_ref[gi], st_ref[gi])
        y_ref[:, xs] = y.astype(y_ref.dtype)
        st_ref[gi] = st_new

    px_ref[...] = x_ref[q - 8:, :]
    pb_ref[...] = b_ref[q - 8:, :]
    pc_ref[...] = c_ref[q - 8:, :]

    @pl.when(c == pl.num_programs(1) - 1)
    def _():
        for gi in range(SSD_GROUPS):
            sfin_ref[GROUP_W * gi:GROUP_W * (gi + 1), :] = st_ref[gi].T
        hist_ref[:, 0:D_INNER] = x_ref[q - 3:, :]
        hist_ref[:, D_INNER:D_INNER + BC_W] = b_ref[q - 3:, :]
        hist_ref[:, D_INNER + BC_W:] = c_ref[q - 3:, :]


def _ssd_scan_prompt(zx, conv_w, conv_b, dtc, csc, dtr, csr, dskip, ng):
    q = SSD_CHUNK
    nc = SEQ // q
    b_blk = D_INNER // BC_W
    small_c = pl.BlockSpec((q, SSD_HEADS), lambda b, c: (b * nc + c, 0))
    small_r = pl.BlockSpec((SSD_HEADS, q), lambda b, c: (0, b * nc + c))
    pspec = pl.BlockSpec((SSD_GROUPS, 1, GROUP_W), lambda b, c: (0, 0, 0))

    def rows(width, col_block):
        return pl.BlockSpec((q, width), lambda b, c: (b * nc + c, col_block))

    def par(nrows, width, col_block):
        return pl.BlockSpec((nrows, width), lambda b, c: (0, col_block))

    return pl.pallas_call(
        _ssd_scan_kernel,
        grid=(BATCH, nc),
        in_specs=[rows(D_INNER, 1), rows(BC_W, 2 * b_blk), rows(BC_W, 2 * b_blk + 1), rows(D_INNER, 0),
                  par(4, D_INNER, 0), par(4, BC_W, b_blk), par(4, BC_W, b_blk + 1),
                  par(1, D_INNER, 0), par(1, BC_W, b_blk), par(1, BC_W, b_blk + 1),
                  small_c, small_c, small_r, small_r, pspec, pspec],
        out_specs=[rows(D_INNER, 0),
                   pl.BlockSpec((D_INNER, SSD_STATE), lambda b, c: (b, 0)),
                   pl.BlockSpec((None, 3, SSD_CONV_DIM), lambda b, c: (b, 0, 0))],
        out_shape=[jax.ShapeDtypeStruct((T_P, D_INNER), BF16),
                   jax.ShapeDtypeStruct((BATCH * D_INNER, SSD_STATE), F32),
                   jax.ShapeDtypeStruct((BATCH, 3, SSD_CONV_DIM), F32)],
        scratch_shapes=[pltpu.VMEM((SSD_GROUPS, SSD_STATE, GROUP_W), F32),
                        pltpu.VMEM((8, D_INNER), F32), pltpu.VMEM((8, BC_W), F32), pltpu.VMEM((8, BC_W), F32)],
        compiler_params=_cparams("arbitrary", "arbitrary"),
        name="ssd_scan",
    )(zx, zx, zx, zx, conv_w, conv_w, conv_w, conv_b, conv_b, conv_b, dtc, csc, dtr, csr, dskip, ng)


def _ssd_step_kernel(s_ref, xt_ref, dtt_ref, bt_ref, ce_ref, dec_ref, so_ref, yt_ref):
    tile_of_lane = lax.broadcasted_iota(jnp.int32, (N_TILES, D_INNER), 1) // LANES
    row32 = lax.broadcasted_iota(jnp.int32, (N_TILES, D_INNER), 0)
    lane32 = lax.broadcasted_iota(jnp.int32, (SSD_STATE, N_TILES), 1)
    for b in range(SSD_STEP_SEQS):
        xdt = (xt_ref[b] * dtt_ref[b]).astype(BF16)
        b_big = jnp.where(tile_of_lane == row32, bt_ref[b], 0.0).astype(BF16)
        upd = jnp.dot(xdt, b_big, preferred_element_type=F32)

        ce = ce_ref[b]
        s_tiles, c_tiles = [], []
        for r in range(N_TILES):
            dtile = jnp.concatenate(
                [jnp.broadcast_to(dec_ref[b, 2 * r:2 * r + 1, :], (SSD_HEAD_DIM, SSD_STATE)),
                 jnp.broadcast_to(dec_ref[b, 2 * r + 1:2 * r + 2, :], (SSD_HEAD_DIM, SSD_STATE))], axis=0)
            s_n = s_ref[b, LANES * r:LANES * (r + 1), :] * dtile + upd[:, LANES * r:LANES * (r + 1)]
            so_ref[b, LANES * r:LANES * (r + 1), :] = s_n
            s_tiles.append(s_n.astype(BF16))
            c_tiles.append(jnp.where(lane32 == r, ce, 0.0).astype(BF16))
        s_big = jnp.concatenate(s_tiles, axis=1)
        c_big = jnp.concatenate(c_tiles, axis=0)
        yt_ref[b] = jnp.dot(s_big, c_big, preferred_element_type=F32)


def _ssd_step_sample(state, x_t, dt_t, b_tiled, c_exp, dec):
    nb = SSD_STEP_SEQS
    tspec = pl.BlockSpec((nb, LANES, N_TILES), lambda b: (b, 0, 0))
    sspec = pl.BlockSpec((nb, D_INNER, SSD_STATE), lambda b: (b, 0, 0))
    return pl.pallas_call(
        _ssd_step_kernel,
        grid=(DEC_BATCH // nb,),
        in_specs=[sspec, tspec, tspec,
                  pl.BlockSpec((nb, 1, D_INNER), lambda b: (b, 0, 0)),
                  pl.BlockSpec((nb, SSD_STATE, N_TILES), lambda b: (b, 0, 0)),
                  pl.BlockSpec((nb, SSD_HEADS, SSD_STATE), lambda b: (b, 0, 0))],
        out_specs=[sspec, tspec],
        out_shape=[jax.ShapeDtypeStruct((DEC_BATCH, D_INNER, SSD_STATE), F32),
                   jax.ShapeDtypeStruct((DEC_BATCH, LANES, N_TILES), F32)],
        compiler_params=_cparams("arbitrary"),
        name="ssd_step",
    )(state, x_t, dt_t, b_tiled, c_exp, dec)


def _gate_norm_kernel(y_ref, x_ref, z_ref, dsk_ref, ng_ref, o_ref):
    y = y_ref[...] + dsk_ref[...] * x_ref[...]
    o_ref[...] = _gate_norm(y, z_ref[...], ng_ref[...]).astype(o_ref.dtype)


def _gate_norm_sample(y_s, xbc_c_s, zx_s, dskip, ng):
    gspec = pl.BlockSpec((T_S, GROUP_W), lambda g: (0, g))
    pspec = pl.BlockSpec((None, 1, GROUP_W), lambda g: (g, 0, 0))
    return pl.pallas_call(
        _gate_norm_kernel,
        grid=(SSD_GROUPS,),
        in_specs=[gspec, gspec, gspec, pspec, pspec],
        out_specs=gspec,
        out_shape=jax.ShapeDtypeStruct((T_S, D_INNER), BF16),
        compiler_params=_cparams("arbitrary"),
        name="ssd_gate_norm",
    )(y_s, xbc_c_s, zx_s, dskip, ng)


def _xattn_prompt_kernel(q_ref, k_ref, v_ref, o_ref):
    scale = MEM_HEAD_DIM ** -0.5
    for h in range(MEM_HEADS):
        sl = slice(MEM_HEAD_DIM * h, MEM_HEAD_DIM * (h + 1))
        kh = k_ref[:, sl].astype(BF16)
        vh = v_ref[:, sl].astype(BF16)
        s = lax.dot_general(q_ref[:, sl], kh, (((1,), (1,)), ((), ())), preferred_element_type=F32) * scale
        m = jnp.max(s, axis=-1, keepdims=True)
        e = jnp.exp(s - m)
        p = (e / jnp.sum(e, axis=-1, keepdims=True)).astype(BF16)
        o_ref[:, sl] = jnp.dot(p, vh, preferred_element_type=F32).astype(o_ref.dtype)


def _xattn_prompt(q, k, v, *, tq=512):
    nq = SEQ // tq
    kspec = pl.BlockSpec((MEM_LEN, D_MODEL), lambda b, i: (b, 0))
    return pl.pallas_call(
        _xattn_prompt_kernel,
        grid=(BATCH, nq),
        in_specs=[pl.BlockSpec((tq, D_MODEL), lambda b, i: (b * nq + i, 0)), kspec, kspec],
        out_specs=pl.BlockSpec((tq, D_MODEL), lambda b, i: (b * nq + i, 0)),
        out_shape=jax.ShapeDtypeStruct((T_P, D_MODEL), BF16),
        compiler_params=_cparams("arbitrary", "arbitrary"),
        name="xattn_prompt",
    )(q, k, v)


def _xattn_sample_kernel(q_ref, k_ref, v_ref, o_ref, *, bb):
    scale = MEM_HEAD_DIM ** -0.5
    for b in range(bb):
        for h in range(MEM_HEADS):
            kh = k_ref[b, :, h, :]
            s = jnp.sum(kh * q_ref[b, h:h + 1, :], axis=-1, keepdims=True) * scale
            m = jnp.max(s, axis=0, keepdims=True)
            e = jnp.exp(s - m)
            p = e / jnp.sum(e, axis=0, keepdims=True)
            o_ref[b, h:h + 1, :] = jnp.sum(p * v_ref[b, :, h, :], axis=0, keepdims=True)


def _xattn_sample(q_s, kc, vc, layer, *, bb=4):
    cspec = pl.BlockSpec((None, bb, MEM_LEN, MEM_HEADS, MEM_HEAD_DIM), lambda i: (layer, i, 0, 0, 0))
    qspec = pl.BlockSpec((bb, MEM_HEADS, MEM_HEAD_DIM), lambda i: (i, 0, 0))
    return pl.pallas_call(
        functools.partial(_xattn_sample_kernel, bb=bb),
        grid=(DEC_BATCH // bb,),
        in_specs=[qspec, cspec, cspec],
        out_specs=qspec,
        out_shape=jax.ShapeDtypeStruct((DEC_BATCH, MEM_HEADS, MEM_HEAD_DIM), F32),
        compiler_params=_cparams("arbitrary"),
        name="xattn_sample",
    )(q_s, kc, vc)


def _mem_attn_ffn_block(hp, hs, hq, i, norm_ffn, xa_w_q, xa_w_o, k_p, v_p, kc, vc, w_gate, w_up, w_down):
    q_p, q_s = _matmul2(hq[0], hq[1], [xa_w_q], i, tn=1024, tm=1024, out_dtype=BF16)
    o_p = _xattn_prompt(q_p, k_p, v_p)
    o_s = _xattn_sample(q_s.astype(F32).reshape(DEC_BATCH, MEM_HEADS, MEM_HEAD_DIM), kc, vc, i)
    o_s = o_s.reshape(DEC_BATCH, D_MODEL).astype(BF16)
    hp, hs, hf_p, hf_s = _matmul_res_norm(o_p, o_s, xa_w_o, i, (hp, hs), norm_ffn[i])
    a_p, a_s = _matmul2(hf_p, hf_s, [w_gate, w_up], i, tn=512, tm=2048, out_dtype=BF16, epilogue=_swiglu_epilogue)
    return _matmul2(a_p, a_s, [w_down], i, tn=512, tm=512, res=(hp, hs))


def _short_conv_block(hp, hs, a, norm_g, norm_next, cache_sc, sc_w_in, sc_w_conv, sc_w_out):
    hn_p, hn_s = _rmsnorm2(hp, hs, norm_g, BF16)
    h0, h1 = cache_sc[a, :, 0, :], cache_sc[a, :, 1, :]
    y_p, y_s, u_s, hist_p = _sc_in_conv(hn_p, hn_s, sc_w_in, a, sc_w_conv[a], h0, h1)
    hp, hs, hq_p, hq_s = _matmul_res_norm(y_p, y_s, sc_w_out, a, (hp, hs), norm_next)
    return hp, hs, (hq_p, hq_s), hist_p, jnp.stack([h1, u_s], axis=1)


def _ssd_block(hp, hs, j, norm_g, conv_hist, state, ssd_w_in, conv_w, conv_b, dt_bias, a_log, d_skip, norm_ssd, w_out):
    hn_p, hn_s = _rmsnorm2(hp, hs, norm_g, BF16)
    w_in_t = jnp.swapaxes(ssd_w_in, 1, 2)
    zx_p, zx_s = _matmul2(hn_p, hn_s, [w_in_t], j, tn=1024, tm=1024, n_out=D_INNER + SSD_CONV_DIM,
                          w_transposed=True)

    dt_p, cs_p, dt_s, dec_s = _dt_prep(hn_p, hn_s, w_in_t, j, dt_bias[j], a_log[j])

    cw, cbias = conv_w[j], conv_b[j].reshape(1, SSD_CONV_DIM)
    ch = conv_hist[j]
    xbc_c_s, raw_s = _ssdconv_sample(zx_s, ch[:, 0, :], ch[:, 1, :], ch[:, 2, :], cw, cbias)
    chist_s = jnp.stack([ch[:, 1, :], ch[:, 2, :], raw_s], axis=1)

    dskip = jnp.repeat(d_skip[j], SSD_HEAD_DIM).reshape(SSD_GROUPS, 1, GROUP_W)
    ng = norm_ssd[j].reshape(SSD_GROUPS, 1, GROUP_W)

    g_p, sfin, chist_p = _ssd_scan_prompt(zx_p, cw, cbias, dt_p, cs_p, dt_p.T, cs_p.T, dskip, ng)
    ss_p = sfin.reshape(BATCH, SSD_HEADS, SSD_HEAD_DIM, SSD_STATE)

    def tile_major(v):
        return jnp.transpose(v.reshape(DEC_BATCH, N_TILES, LANES), (0, 2, 1))

    tiles_per_group = GROUP_W // LANES
    x_t = tile_major(xbc_c_s[:, :D_INNER])
    dt_t = tile_major(jnp.repeat(dt_s[:, :SSD_HEADS], SSD_HEAD_DIM, axis=1))
    b_g = xbc_c_s[:, D_INNER:D_INNER + BC_W].reshape(DEC_BATCH, SSD_GROUPS, SSD_STATE)
    c_g = xbc_c_s[:, D_INNER + BC_W:].reshape(DEC_BATCH, SSD_GROUPS, SSD_STATE)
    b_tiled = jnp.repeat(b_g, tiles_per_group, axis=1).reshape(DEC_BATCH, 1, D_INNER)
    c_exp = jnp.transpose(jnp.repeat(c_g, tiles_per_group, axis=1), (0, 2, 1))
    dec_b = jnp.broadcast_to(dec_s[:, :SSD_HEADS, None], (DEC_BATCH, SSD_HEADS, SSD_STATE))
    s_new, y_t = _ssd_step_sample(state[j].reshape(DEC_BATCH, D_INNER, SSD_STATE), x_t, dt_t, b_tiled, c_exp, dec_b)
    y_s = jnp.transpose(y_t, (0, 2, 1)).reshape(DEC_BATCH, D_INNER)
    g_s = _gate_norm_sample(y_s, xbc_c_s, zx_s, dskip, ng)
    ss_s = s_new.reshape(DEC_BATCH, SSD_HEADS, SSD_HEAD_DIM, SSD_STATE)

    hp, hs = _matmul2(g_p, g_s, [w_out], j, tn=512, tm=1024, res=(hp, hs))
    return hp, hs, chist_p, chist_s, ss_p, ss_s


def kernel(x_prompt, x_sample, mem_prompt, cache_sc, state_ssd_conv, state_ssd, cache_mem_k, cache_mem_v, norm_mix, norm_mem_q, norm_mem_kv, norm_ffn, norm_final, sc_w_in, sc_w_conv, sc_w_out, ssd_w_in, ssd_conv_w, ssd_conv_b, ssd_dt_bias, ssd_a_log, ssd_d, ssd_norm, ssd_w_out, xa_w_q, xa_w_k, xa_w_v, xa_w_o, ffn_w_gate, ffn_w_up, ffn_w_down):
    depth = norm_mix.shape[0]
    hp = x_prompt.reshape(T_P, D_MODEL)
    hs = x_sample.reshape(T_S, D_MODEL)

    mem = mem_prompt.reshape(BATCH * MEM_LEN, D_MODEL)
    k_ps, v_ps = [], []
    for i in range(depth):
        m = _rmsnorm1(mem, norm_mem_kv[i], BF16, tm=512)
        k_ps.append(_matmul2(m, None, [xa_w_k], i, tn=1024, tm=512))
        v_ps.append(_matmul2(m, None, [xa_w_v], i, tn=1024, tm=512))

    sc_p, sc_s, sconv_p, sconv_s, ss_p, ss_s = [], [], [], [], [], []
    for i in range(depth):
        if i % 2 == 0:
            hp, hs, hq, a_p, a_s = _short_conv_block(hp, hs, i // 2, norm_mix[i], norm_mem_q[i], cache_sc,
                                                     sc_w_in, sc_w_conv, sc_w_out)
            sc_p.append(a_p)
            sc_s.append(a_s)
        else:
            hp, hs, c_p, c_s, s_p, s_s = _ssd_block(hp, hs, i // 2, norm_mix[i], state_ssd_conv, state_ssd, ssd_w_in,
                                                    ssd_conv_w, ssd_conv_b, ssd_dt_bias, ssd_a_log, ssd_d, ssd_norm,
                                                    ssd_w_out)
            sconv_p.append(c_p)
            sconv_s.append(c_s)
            ss_p.append(s_p)
            ss_s.append(s_s)
            hq = _rmsnorm2(hp, hs, norm_mem_q[i], BF16)
        hp, hs = _mem_attn_ffn_block(hp, hs, hq, i, norm_ffn, xa_w_q, xa_w_o, k_ps[i], v_ps[i],
                                     cache_mem_k, cache_mem_v, ffn_w_gate, ffn_w_up, ffn_w_down)

    y_p, y_s = _rmsnorm2(hp, hs, norm_final, F32)

    kv_shape = (depth, BATCH, MEM_LEN, MEM_HEADS, MEM_HEAD_DIM)
    return (y_p.reshape(BATCH, SEQ, D_MODEL), y_s.reshape(DEC_BATCH, 1, D_MODEL),
            jnp.stack(sc_p), jnp.stack(sc_s), jnp.stack(sconv_p), jnp.stack(sconv_s),
            jnp.stack(ss_p), jnp.stack(ss_s),
            jnp.stack(k_ps).reshape(kv_shape), jnp.stack(v_ps).reshape(kv_shape))
```

```python
import functools

import jax
import jax.numpy as jnp
from jax import lax
from jax.experimental import pallas as pl
from jax.experimental.pallas import tpu as pltpu

F32 = jnp.float32
BF16 = jnp.bfloat16

D_MODEL = 2048
BATCH = 4
SEQ = 2048
DEC_BATCH = 128
T_P = BATCH * SEQ
T_S = DEC_BATCH
D_INNER = 4096
SSD_HEAD_DIM = 64
SSD_HEADS = 64
SSD_GROUPS = 8
SSD_HPG = 8
SSD_STATE = 128
GROUP_W = SSD_HPG * SSD_HEAD_DIM
BC_W = SSD_GROUPS * SSD_STATE
SSD_CONV_DIM = D_INNER + 2 * BC_W
MEM_LEN = 256
MEM_HEADS = 4
MEM_HEAD_DIM = 512
RMS_EPS = 1e-5
LANES = 128

V7X_VMEM_LIMIT_BYTES = 56 * 1024 * 1024
SSD_CHUNK = 128
SSD_STEP_SEQS = 4
N_TILES = D_INNER // LANES


def _cparams(*sem):
    return pltpu.CompilerParams(dimension_semantics=sem, vmem_limit_bytes=V7X_VMEM_LIMIT_BYTES)


def _silu(x):
    hx = 0.5 * x
    return hx * (1.0 + jnp.tanh(hx))


def _rms(x, g):
    ms = jnp.mean(x * x, axis=-1, keepdims=True)
    return x * lax.rsqrt(ms + RMS_EPS) * g


def _rmsnorm_kernel(xp_ref, xs_ref, g_ref, op_ref, os_ref, *, n_p):
    i = pl.program_id(0)

    @pl.when(i < n_p)
    def _():
        op_ref[...] = _rms(xp_ref[...], g_ref[...]).astype(op_ref.dtype)

    @pl.when(i == n_p)
    def _():
        os_ref[...] = _rms(xs_ref[...], g_ref[...]).astype(os_ref.dtype)


def _rmsnorm2(xp, xs, g, out_dtype, *, tm=1024):
    d = xp.shape[1]
    n_p = xp.shape[0] // tm
    pspec = pl.BlockSpec((tm, d), lambda i: (jnp.minimum(i, n_p - 1), 0))
    sspec = pl.BlockSpec((T_S, d), lambda i: (0, 0))
    return pl.pallas_call(
        functools.partial(_rmsnorm_kernel, n_p=n_p),
        grid=(n_p + 1,),
        in_specs=[pspec, sspec, pl.BlockSpec((1, d), lambda i: (0, 0))],
        out_specs=[pspec, sspec],
        out_shape=[jax.ShapeDtypeStruct(xp.shape, out_dtype), jax.ShapeDtypeStruct(xs.shape, out_dtype)],
        compiler_params=_cparams("arbitrary"),
        name="rmsnorm",
    )(xp, xs, g.reshape(1, d))


def _mem_proj_kernel(x_ref, g_ref, w_ref, o2_ref, o5_ref, wb_ref, *, seqs):
    @pl.when(pl.program_id(1) == 0)
    def _():
        wb_ref[...] = w_ref[...].astype(BF16)

    m = _rms(x_ref[...], g_ref[...]).astype(BF16)
    acc = jnp.dot(m, wb_ref[...], preferred_element_type=F32)
    o2_ref[...] = acc
    for s in range(seqs):
        for h in range(MEM_HEADS):
            o5_ref[s, :, h, :] = acc[MEM_LEN * s:MEM_LEN * (s + 1), MEM_HEAD_DIM * h:MEM_HEAD_DIM * (h + 1)]


def _mem_proj(mem, norm_g, w3, *, seqs=2):
    depth = w3.shape[0]
    rows, d = mem.shape
    tm = seqs * MEM_LEN
    return pl.pallas_call(
        functools.partial(_mem_proj_kernel, seqs=seqs),
        grid=(depth, rows // tm),
        in_specs=[pl.BlockSpec((tm, d), lambda l, i: (i, 0)),
                  pl.BlockSpec((None, 1, d), lambda l, i: (l, 0, 0)),
                  pl.BlockSpec((None, d, d), lambda l, i: (l, 0, 0), pipeline_mode=pl.Buffered(1))],
        out_specs=[pl.BlockSpec((None, tm, d), lambda l, i: (l, i, 0)),
                   pl.BlockSpec((None, seqs, MEM_LEN, MEM_HEADS, MEM_HEAD_DIM), lambda l, i: (l, i, 0, 0, 0))],
        out_shape=[jax.ShapeDtypeStruct((depth, rows, d), F32),
                   jax.ShapeDtypeStruct((depth, rows // MEM_LEN, MEM_LEN, MEM_HEADS, MEM_HEAD_DIM), F32)],
        scratch_shapes=[pltpu.VMEM((d, d), BF16)],
        compiler_params=_cparams("arbitrary", "arbitrary"),
        name="mem_proj",
    )(mem, norm_g.reshape(depth, 1, d), w3)


_NT_DIMS = (((1,), (1,)), ((), ()))


def _mm_kernel(*refs, has_s, has_res, n_w, epilogue, w_transposed):
    it = iter(refs)
    xp_ref = next(it)
    xs_ref = next(it) if has_s else None
    w_refs = [next(it) for _ in range(n_w)]
    rp_ref = next(it) if has_res else None
    rs_ref = next(it) if (has_res and has_s) else None
    op_ref = next(it)
    os_ref = next(it) if has_s else None
    wb_refs = [next(it) for _ in range(n_w)]

    def run(x_ref, r_ref, o_ref):
        x = x_ref[...]
        if w_transposed:
            accs = [lax.dot_general(x, wb_ref[...], _NT_DIMS, preferred_element_type=F32) for wb_ref in wb_refs]
        else:
            accs = [jnp.dot(x, wb_ref[...], preferred_element_type=F32) for wb_ref in wb_refs]
        out = epilogue(*accs)
        if r_ref is not None:
            out = out + r_ref[...]
        o_ref[...] = out.astype(o_ref.dtype)

    @pl.when(pl.program_id(1) == 0)
    def _():
        for w_ref, wb_ref in zip(w_refs, wb_refs):
            wb_ref[...] = w_ref[...].astype(BF16)
        if has_s:
            run(xs_ref, rs_ref, os_ref)

    run(xp_ref, rp_ref, op_ref)


def _identity(a):
    return a


def _swiglu_epilogue(a, u):
    return _silu(a) * u


def _matmul2(xp, xs, ws, layer, *, tn, tm, n_out=None, res=None, out_dtype=F32, epilogue=_identity,
             w_transposed=False):
    m, k = xp.shape
    n = n_out if n_out is not None else ws[0].shape[1 if w_transposed else 2]
    has_s = xs is not None
    xp_spec = pl.BlockSpec((tm, k), lambda j, i: (i, 0))
    xs_spec = pl.BlockSpec((T_S, k), lambda j, i: (0, 0))
    if w_transposed:
        w_spec = pl.BlockSpec((None, tn, k), lambda j, i: (layer, j, 0))
    else:
        w_spec = pl.BlockSpec((None, k, tn), lambda j, i: (layer, 0, j))
    op_spec = pl.BlockSpec((tm, tn), lambda j, i: (i, j))
    os_spec = pl.BlockSpec((T_S, tn), lambda j, i: (0, j))

    in_specs, args = [xp_spec], [xp]
    if has_s:
        in_specs.append(xs_spec)
        args.append(xs)
    in_specs += [w_spec] * len(ws)
    args += list(ws)
    if res is not None:
        in_specs.append(op_spec)
        args.append(res[0])
        if has_s:
            in_specs.append(os_spec)
            args.append(res[1])
    out_specs = [op_spec] + ([os_spec] if has_s else [])
    out_shape = [jax.ShapeDtypeStruct((m, n), out_dtype)] + ([jax.ShapeDtypeStruct((T_S, n), out_dtype)] if has_s else [])
    outs = pl.pallas_call(
        functools.partial(_mm_kernel, has_s=has_s, has_res=res is not None, n_w=len(ws), epilogue=epilogue,
                          w_transposed=w_transposed),
        grid=(n // tn, m // tm),
        in_specs=in_specs,
        out_specs=out_specs,
        out_shape=out_shape,
        scratch_shapes=[pltpu.VMEM((tn, k) if w_transposed else (k, tn), BF16) for _ in ws],
        compiler_params=_cparams("arbitrary", "arbitrary"),
        name="matmul",
    )(*args)
    return tuple(outs) if has_s else outs[0]


def _mm_res_norm_kernel(xp_ref, xs_ref, w_ref, rp_ref, rs_ref, g_ref, hp_ref, hs_ref, np_ref, ns_ref, wb_ref):
    def run(x_ref, r_ref, h_ref, n_ref):
        h = jnp.dot(x_ref[...], wb_ref[...], preferred_element_type=F32) + r_ref[...]
        h_ref[...] = h
        n_ref[...] = _rms(h, g_ref[...]).astype(n_ref.dtype)

    @pl.when(pl.program_id(0) == 0)
    def _():
        wb_ref[...] = w_ref[...].astype(BF16)
        run(xs_ref, rs_ref, hs_ref, ns_ref)

    run(xp_ref, rp_ref, hp_ref, np_ref)


def _matmul_res_norm(xp, xs, w3, layer, res, g, *, tm=512):
    m, k = xp.shape
    n = w3.shape[2]
    pspec = pl.BlockSpec((tm, n), lambda i: (i, 0))
    sspec = pl.BlockSpec((T_S, n), lambda i: (0, 0))
    wspec = pl.BlockSpec((None, k, n), lambda i: (layer, 0, 0), pipeline_mode=pl.Buffered(1))
    return pl.pallas_call(
        _mm_res_norm_kernel,
        grid=(m // tm,),
        in_specs=[pl.BlockSpec((tm, k), lambda i: (i, 0)), pl.BlockSpec((T_S, k), lambda i: (0, 0)),
                  wspec, pspec, sspec, pl.BlockSpec((1, n), lambda i: (0, 0))],
        out_specs=[pspec, sspec, pspec, sspec],
        out_shape=[jax.ShapeDtypeStruct((m, n), F32), jax.ShapeDtypeStruct((T_S, n), F32),
                   jax.ShapeDtypeStruct((m, n), BF16), jax.ShapeDtypeStruct((T_S, n), BF16)],
        scratch_shapes=[pltpu.VMEM((k, n), BF16)],
        compiler_params=_cparams("arbitrary"),
        name="matmul_res_norm",
    )(xp, xs, w3, res[0], res[1], g.reshape(1, n))


def _shifted_rows(u, prev8, k):
    row8 = lax.broadcasted_iota(jnp.int32, prev8.shape, 0)
    r = pltpu.roll(u, k, axis=0)
    top = jnp.where(row8 < k, pltpu.roll(prev8, k, axis=0), r[0:8, :])
    return jnp.concatenate([top, r[8:, :]], axis=0)


def _sc_in_kernel(xp_ref, xs_ref, wb_ref, wc_ref, wv_ref, cw_ref, h0_ref, h1_ref,
                  yp_ref, ys_ref, us_ref, hist_ref, wbb_ref, wcb_ref, wvb_ref, prev_ref, *, tiles_per_seq):
    i = pl.program_id(1)
    cw = cw_ref[...]

    def gates(x):
        return (jnp.dot(x, wbb_ref[...], preferred_element_type=F32),
                jnp.dot(x, wcb_ref[...], preferred_element_type=F32),
                jnp.dot(x, wvb_ref[...], preferred_element_type=F32))

    @pl.when(i == 0)
    def _():
        wbb_ref[...] = wb_ref[...].astype(BF16)
        wcb_ref[...] = wc_ref[...].astype(BF16)
        wvb_ref[...] = wv_ref[...].astype(BF16)
        bg, cg, v = gates(xs_ref[...])
        u = cg * v
        conv = cw[2:3, :] * u + cw[1:2, :] * h1_ref[...] + cw[0:1, :] * h0_ref[...]
        ys_ref[...] = (bg * conv).astype(ys_ref.dtype)
        us_ref[...] = u

    @pl.when(i % tiles_per_seq == 0)
    def _():
        prev_ref[...] = jnp.zeros_like(prev_ref)

    bg, cg, v = gates(xp_ref[...])
    u = cg * v
    prev8 = prev_ref[...]
    conv = cw[2:3, :] * u + cw[1:2, :] * _shifted_rows(u, prev8, 1) + cw[0:1, :] * _shifted_rows(u, prev8, 2)
    yp_ref[...] = (bg * conv).astype(yp_ref.dtype)
    tm = u.shape[0]
    prev_ref[...] = u[tm - 8:, :]

    @pl.when(i % tiles_per_seq == tiles_per_seq - 1)
    def _():
        hist_ref[...] = u[tm - 2:, :]


def _sc_in_conv(xp, xs, w_in3, layer, w_conv, h0, h1, *, tn=512, tm=1024):
    k = xp.shape[1]
    nb = D_MODEL // tn
    tiles_per_seq = SEQ // tm

    def wspec(part):
        return pl.BlockSpec((None, k, tn), lambda j, i: (layer, 0, j + part * nb))

    sspec = pl.BlockSpec((T_S, tn), lambda j, i: (0, j))
    pspec = pl.BlockSpec((tm, tn), lambda j, i: (i, j))
    return pl.pallas_call(
        functools.partial(_sc_in_kernel, tiles_per_seq=tiles_per_seq),
        grid=(nb, T_P // tm),
        in_specs=[pl.BlockSpec((tm, k), lambda j, i: (i, 0)),
                  pl.BlockSpec((T_S, k), lambda j, i: (0, 0)),
                  wspec(0), wspec(1), wspec(2),
                  pl.BlockSpec((3, tn), lambda j, i: (0, j)), sspec, sspec],
        out_specs=[pspec, sspec, sspec,
                   pl.BlockSpec((None, 2, tn), lambda j, i: (i // tiles_per_seq, 0, j))],
        out_shape=[jax.ShapeDtypeStruct((T_P, D_MODEL), BF16),
                   jax.ShapeDtypeStruct((T_S, D_MODEL), BF16),
                   jax.ShapeDtypeStruct((T_S, D_MODEL), F32),
                   jax.ShapeDtypeStruct((BATCH, 2, D_MODEL), F32)],
        scratch_shapes=[pltpu.VMEM((k, tn), BF16), pltpu.VMEM((k, tn), BF16), pltpu.VMEM((k, tn), BF16),
                        pltpu.VMEM((8, tn), F32)],
        compiler_params=_cparams("arbitrary", "arbitrary"),
        name="sc_in_conv",
    )(xp, xs, w_in3, w_in3, w_in3, w_conv, h0, h1)


def _ssdconv_sample_kernel(x_ref, h0_ref, h1_ref, h2_ref, w_ref, b_ref, o_ref, raw_ref):
    x = x_ref[...]
    w = w_ref[...]
    conv = w[3:4, :] * x + w[2:3, :] * h2_ref[...] + w[1:2, :] * h1_ref[...] + w[0:1, :] * h0_ref[...]
    o_ref[...] = _silu(conv + b_ref[...])
    raw_ref[...] = x


def _ssdconv_sample(zx, h0, h1, h2, conv_w, conv_b, *, tc=512):
    nb = SSD_CONV_DIM // tc
    off = D_INNER // tc
    hspec = pl.BlockSpec((T_S, tc), lambda j: (0, j))
    return pl.pallas_call(
        _ssdconv_sample_kernel,
        grid=(nb,),
        in_specs=[pl.BlockSpec((T_S, tc), lambda j: (0, j + off)), hspec, hspec, hspec,
                  pl.BlockSpec((4, tc), lambda j: (0, j)),
                  pl.BlockSpec((1, tc), lambda j: (0, j))],
        out_specs=[hspec, hspec],
        out_shape=[jax.ShapeDtypeStruct((T_S, SSD_CONV_DIM), F32),
                   jax.ShapeDtypeStruct((T_S, SSD_CONV_DIM), F32)],
        compiler_params=_cparams("arbitrary"),
        name="ssdconv_sample",
    )(zx, h0, h1, h2, conv_w, conv_b)


def _dt_kernel(xp_ref, xs_ref, w_ref, bias_ref, alog_ref, dtp_ref, csp_ref, dts_ref, decs_ref, *, n_p):
    i = pl.program_id(0)

    def dt_da(x_ref):
        w = w_ref[:SSD_HEADS, :].astype(BF16)
        raw = lax.dot_general(x_ref[...], w, _NT_DIMS, preferred_element_type=F32) + bias_ref[...]
        dt = jnp.maximum(raw, 0.0) + jnp.log1p(jnp.exp(-jnp.abs(raw)))
        return dt, dt * (-jnp.exp(alog_ref[...]))

    @pl.when(i < n_p)
    def _():
        dt, da = dt_da(xp_ref)
        dtp_ref[...] = dt
        row_in_chunk = lax.broadcasted_iota(jnp.int32, da.shape, 0) % SSD_CHUNK
        cs = da
        k = 1
        while k < SSD_CHUNK:
            cs = cs + jnp.where(row_in_chunk >= k, pltpu.roll(cs, k, axis=0), 0.0)
            k *= 2
        csp_ref[...] = cs

    @pl.when(i == n_p)
    def _():
        dt, da = dt_da(xs_ref)
        dts_ref[...] = dt
        decs_ref[...] = jnp.exp(da)


def _dt_prep(hn_p, hn_s, w_in3t, layer, dt_bias, a_log):
    k = hn_p.shape[1]
    tm = 8 * SSD_CHUNK
    n_p = T_P // tm
    dt_block = (D_INNER + SSD_CONV_DIM) // LANES
    pin = pl.BlockSpec((tm, k), lambda i: (jnp.minimum(i, n_p - 1), 0))
    pout = pl.BlockSpec((tm, SSD_HEADS), lambda i: (jnp.minimum(i, n_p - 1), 0))
    sout = pl.BlockSpec((T_S, SSD_HEADS), lambda i: (0, 0))
    one = pl.BlockSpec((1, SSD_HEADS), lambda i: (0, 0))
    return pl.pallas_call(
        functools.partial(_dt_kernel, n_p=n_p),
        grid=(n_p + 1,),
        in_specs=[pin, pl.BlockSpec((T_S, k), lambda i: (0, 0)),
                  pl.BlockSpec((None, LANES, k), lambda i: (layer, dt_block, 0)), one, one],
        out_specs=[pout, pout, sout, sout],
        out_shape=[jax.ShapeDtypeStruct((T_P, SSD_HEADS), F32), jax.ShapeDtypeStruct((T_P, SSD_HEADS), F32),
                   jax.ShapeDtypeStruct((T_S, SSD_HEADS), F32), jax.ShapeDtypeStruct((T_S, SSD_HEADS), F32)],
        compiler_params=_cparams("arbitrary"),
        name="ssd_dt",
    )(hn_p, hn_s, w_in3t, dt_bias.reshape(1, SSD_HEADS), a_log.reshape(1, SSD_HEADS))


def _expand_heads(v, onehot):
    hi = v.astype(BF16)
    lo = (v - hi.astype(F32)).astype(BF16)
    return (jnp.dot(hi, onehot, preferred_element_type=F32)
            + jnp.dot(lo, onehot, preferred_element_type=F32))


def _gate_norm(y, z, ng):
    return _rms(y * _silu(z), ng)


def _ssd_group_chunk(x, bm, cm, z, dtc, csc, dtr, csr, dsk, ng, st):
    q = SSD_CHUNK
    xb = x.astype(BF16)
    bb = bm.astype(BF16)
    cb = cm.astype(BF16)

    lane = lax.broadcasted_iota(jnp.int32, (SSD_HPG, GROUP_W), 1)
    head = lax.broadcasted_iota(jnp.int32, (SSD_HPG, GROUP_W), 0)
    onehot = jnp.where(lane // SSD_HEAD_DIM == head, 1.0, 0.0).astype(BF16)

    cbm = lax.dot_general(cb, bb, (((1,), (1,)), ((), ())), preferred_element_type=F32)
    row = lax.broadcasted_iota(jnp.int32, (q, q), 0)
    col = lax.broadcasted_iota(jnp.int32, (q, q), 1)
    tri = row >= col
    lane128 = lax.broadcasted_iota(jnp.int32, (q, LANES), 1)
    lo_half = lane128 < SSD_HEAD_DIM

    tiles = []
    for m in range(GROUP_W // LANES):
        xt = xb[:, LANES * m:LANES * (m + 1)]
        acc = None
        for j, xm in ((2 * m, jnp.where(lo_half, xt, 0)), (2 * m + 1, jnp.where(lo_half, 0, xt))):
            seg = csc[:, j:j + 1] - csr[j:j + 1, :]
            dec = jnp.exp(jnp.where(tri, seg, -jnp.inf))
            w = (cbm * dec * dtr[j:j + 1, :]).astype(BF16)
            part = jnp.dot(w, xm, preferred_element_type=F32)
            acc = part if acc is None else acc + part
        tiles.append(acc)
    y_diag = jnp.concatenate(tiles, axis=1)

    e_exp = _expand_heads(jnp.exp(csc), onehot)
    y_off = jnp.dot(cb, st.astype(BF16), preferred_element_type=F32) * e_exp

    cs_last = csc[q - 1:q, :]
    te = _expand_heads(jnp.exp(cs_last - csc) * dtc, onehot)
    xw = (x * te).astype(BF16)
    d_st = jnp.dot(bm.T.astype(BF16), xw, preferred_element_type=F32)
    st_new = st * e_exp[q - 1:q, :] + d_st
    y = y_diag + y_off + dsk * x
    return _gate_norm(y, z, ng), st_new


def _causal_conv4_silu(x, prev8, w, bias):
    acc = w[3:4, :] * x + bias
    for k in (1, 2, 3):
        acc = acc + w[3 - k:4 - k, :] * _shifted_rows(x, prev8, k)
    return _silu(acc)


def _ssd_scan_kernel(x_ref, b_ref, c_ref, z_ref, wx_ref, wb_ref, wc_ref, bx_ref, bb_ref, bc_ref,
                     dtc_ref, csc_ref, dtr_ref, csr_ref, dsk_ref, ng_ref,
                     y_ref, sfin_ref, hist_ref, st_ref, px_ref, pb_ref, pc_ref):
    q = SSD_CHUNK
    c = pl.program_id(1)

    @pl.when(c == 0)
    def _():
        st_ref[...] = jnp.zeros_like(st_ref)
        px_ref[...] = jnp.zeros_like(px_ref)
        pb_ref[...] = jnp.zeros_like(pb_ref)
        pc_ref[...] = jnp.zeros_like(pc_ref)

    for gi in range(SSD_GROUPS):
        xs = slice(GROUP_W * gi, GROUP_W * (gi + 1))
        ns = slice(SSD_STATE * gi, SSD_STATE * (gi + 1))
        hs = slice(SSD_HPG * gi, SSD_HPG * (gi + 1))
        x = _causal_conv4_silu(x_ref[:, xs], px_ref[:, xs], wx_ref[:, xs], bx_ref[:, xs])
        bm = _causal_conv4_silu(b_ref[:, ns], pb_ref[:, ns], wb_ref[:, ns], bb_ref[:, ns])
        cm = _causal_conv4_silu(c_ref[:, ns], pc_ref[:, ns], wc_ref[:, ns], bc_ref[:, ns])
        y, st_new = _ssd_group_chunk(x, bm, cm, z_ref[:, xs],
                                     dtc_ref[:, hs], csc_ref[:, hs], dtr_ref[hs, :], csr_ref[hs, :],
                                     dsk_ref[gi], ng_ref[gi], st_ref[gi])
        y_ref[:, xs] = y.astype(y_ref.dtype)
        st_ref[gi] = st_new

    px_ref[...] = x_ref[q - 8:, :]
    pb_ref[...] = b_ref[q - 8:, :]
    pc_ref[...] = c_ref[q - 8:, :]

    @pl.when(c == pl.num_programs(1) - 1)
    def _():
        for gi in range(SSD_GROUPS):
            sfin_ref[GROUP_W * gi:GROUP_W * (gi + 1), :] = st_ref[gi].T
        hist_ref[:, 0:D_INNER] = x_ref[q - 3:, :]
        hist_ref[:, D_INNER:D_INNER + BC_W] = b_ref[q - 3:, :]
        hist_ref[:, D_INNER + BC_W:] = c_ref[q - 3:, :]


def _ssd_scan_prompt(zx, conv_w, conv_b, dtc, csc, dtr, csr, dskip, ng):
    q = SSD_CHUNK
    nc = SEQ // q
    b_blk = D_INNER // BC_W
    small_c = pl.BlockSpec((q, SSD_HEADS), lambda b, c: (b * nc + c, 0))
    small_r = pl.BlockSpec((SSD_HEADS, q), lambda b, c: (0, b * nc + c))
    pspec = pl.BlockSpec((SSD_GROUPS, 1, GROUP_W), lambda b, c: (0, 0, 0))

    def rows(width, col_block):
        return pl.BlockSpec((q, width), lambda b, c: (b * nc + c, col_block))

    def par(nrows, width, col_block):
        return pl.BlockSpec((nrows, width), lambda b, c: (0, col_block))

    return pl.pallas_call(
        _ssd_scan_kernel,
        grid=(BATCH, nc),
        in_specs=[rows(D_INNER, 1), rows(BC_W, 2 * b_blk), rows(BC_W, 2 * b_blk + 1), rows(D_INNER, 0),
                  par(4, D_INNER, 0), par(4, BC_W, b_blk), par(4, BC_W, b_blk + 1),
                  par(1, D_INNER, 0), par(1, BC_W, b_blk), par(1, BC_W, b_blk + 1),
                  small_c, small_c, small_r, small_r, pspec, pspec],
        out_specs=[rows(D_INNER, 0),
                   pl.BlockSpec((D_INNER, SSD_STATE), lambda b, c: (b, 0)),
                   pl.BlockSpec((None, 3, SSD_CONV_DIM), lambda b, c: (b, 0, 0))],
        out_shape=[jax.ShapeDtypeStruct((T_P, D_INNER), BF16),
                   jax.ShapeDtypeStruct((BATCH * D_INNER, SSD_STATE), F32),
                   jax.ShapeDtypeStruct((BATCH, 3, SSD_CONV_DIM), F32)],
        scratch_shapes=[pltpu.VMEM((SSD_GROUPS, SSD_STATE, GROUP_W), F32),
                        pltpu.VMEM((8, D_INNER), F32), pltpu.VMEM((8, BC_W), F32), pltpu.VMEM((8, BC_W), F32)],
        compiler_params=_cparams("arbitrary", "arbitrary"),
        name="ssd_scan",
    )(zx, zx, zx, zx, conv_w, conv_w, conv_w, conv_b, conv_b, conv_b, dtc, csc, dtr, csr, dskip, ng)


def _ssd_step_kernel(s_ref, xt_ref, dtt_ref, bt_ref, ce_ref, dec_ref, so_ref, yt_ref):
    tile_of_lane = lax.broadcasted_iota(jnp.int32, (N_TILES, D_INNER), 1) // LANES
    row32 = lax.broadcasted_iota(jnp.int32, (N_TILES, D_INNER), 0)
    lane32 = lax.broadcasted_iota(jnp.int32, (SSD_STATE, N_TILES), 1)
    for b in range(SSD_STEP_SEQS):
        xdt = (xt_ref[b] * dtt_ref[b]).astype(BF16)
        b_big = jnp.where(tile_of_lane == row32, bt_ref[b], 0.0).astype(BF16)
        upd = jnp.dot(xdt, b_big, preferred_element_type=F32)

        ce = ce_ref[b]
        s_tiles, c_tiles = [], []
        for r in range(N_TILES):
            dtile = jnp.concatenate(
                [jnp.broadcast_to(dec_ref[b, 2 * r:2 * r + 1, :], (SSD_HEAD_DIM, SSD_STATE)),
                 jnp.broadcast_to(dec_ref[b, 2 * r + 1:2 * r + 2, :], (SSD_HEAD_DIM, SSD_STATE))], axis=0)
            s_n = s_ref[b, LANES * r:LANES * (r + 1), :] * dtile + upd[:, LANES * r:LANES * (r + 1)]
            so_ref[b, LANES * r:LANES * (r + 1), :] = s_n
            s_tiles.append(s_n.astype(BF16))
            c_tiles.append(jnp.where(lane32 == r, ce, 0.0).astype(BF16))
        s_big = jnp.concatenate(s_tiles, axis=1)
        c_big = jnp.concatenate(c_tiles, axis=0)
        yt_ref[b] = jnp.dot(s_big, c_big, preferred_element_type=F32)


def _ssd_step_sample(state, x_t, dt_t, b_tiled, c_exp, dec):
    nb = SSD_STEP_SEQS
    tspec = pl.BlockSpec((nb, LANES, N_TILES), lambda b: (b, 0, 0))
    sspec = pl.BlockSpec((nb, D_INNER, SSD_STATE), lambda b: (b, 0, 0))
    return pl.pallas_call(
        _ssd_step_kernel,
        grid=(DEC_BATCH // nb,),
        in_specs=[sspec, tspec, tspec,
                  pl.BlockSpec((nb, 1, D_INNER), lambda b: (b, 0, 0)),
                  pl.BlockSpec((nb, SSD_STATE, N_TILES), lambda b: (b, 0, 0)),
                  pl.BlockSpec((nb, SSD_HEADS, SSD_STATE), lambda b: (b, 0, 0))],
        out_specs=[sspec, tspec],
        out_shape=[jax.ShapeDtypeStruct((DEC_BATCH, D_INNER, SSD_STATE), F32),
                   jax.ShapeDtypeStruct((DEC_BATCH, LANES, N_TILES), F32)],
        compiler_params=_cparams("arbitrary"),
        name="ssd_step",
    )(state, x_t, dt_t, b_tiled, c_exp, dec)


def _gate_norm_kernel(y_ref, x_ref, z_ref, dsk_ref, ng_ref, o_ref):
    y = y_ref[...] + dsk_ref[...] * x_ref[...]
    o_ref[...] = _gate_norm(y, z_ref[...], ng_ref[...]).astype(o_ref.dtype)


def _gate_norm_sample(y_s, xbc_c_s, zx_s, dskip, ng):
    gspec = pl.BlockSpec((T_S, GROUP_W), lambda g: (0, g))
    pspec = pl.BlockSpec((None, 1, GROUP_W), lambda g: (g, 0, 0))
    return pl.pallas_call(
        _gate_norm_kernel,
        grid=(SSD_GROUPS,),
        in_specs=[gspec, gspec, gspec, pspec, pspec],
        out_specs=gspec,
        out_shape=jax.ShapeDtypeStruct((T_S, D_INNER), BF16),
        compiler_params=_cparams("arbitrary"),
        name="ssd_gate_norm",
    )(y_s, xbc_c_s, zx_s, dskip, ng)


def _xattn_prompt_kernel(q_ref, k_ref, v_ref, o_ref):
    scale = MEM_HEAD_DIM ** -0.5
    for h in range(MEM_HEADS):
        sl = slice(MEM_HEAD_DIM * h, MEM_HEAD_DIM * (h + 1))
        kh = k_ref[:, sl].astype(BF16)
        vh = v_ref[:, sl].astype(BF16)
        s = lax.dot_general(q_ref[:, sl], kh, (((1,), (1,)), ((), ())), preferred_element_type=F32) * scale
        m = jnp.max(s, axis=-1, keepdims=True)
        e = jnp.exp(s - m)
        p = (e / jnp.sum(e, axis=-1, keepdims=True)).astype(BF16)
        o_ref[:, sl] = jnp.dot(p, vh, preferred_element_type=F32).astype(o_ref.dtype)


def _xattn_prompt(q, k, v, layer, *, tq=1024):
    nq = SEQ // tq
    kspec = pl.BlockSpec((None, MEM_LEN, D_MODEL), lambda b, i: (layer, b, 0))
    return pl.pallas_call(
        _xattn_prompt_kernel,
        grid=(BATCH, nq),
        in_specs=[pl.BlockSpec((tq, D_MODEL), lambda b, i: (b * nq + i, 0)), kspec, kspec],
        out_specs=pl.BlockSpec((tq, D_MODEL), lambda b, i: (b * nq + i, 0)),
        out_shape=jax.ShapeDtypeStruct((T_P, D_MODEL), BF16),
        compiler_params=_cparams("arbitrary", "arbitrary"),
        name="xattn_prompt",
    )(q, k, v)


def _xattn_sample_kernel(q_ref, k_ref, v_ref, o_ref, *, bb):
    scale = MEM_HEAD_DIM ** -0.5
    for b in range(bb):
        for h in range(MEM_HEADS):
            kh = k_ref[b, :, h, :]
            s = jnp.sum(kh * q_ref[b, h:h + 1, :], axis=-1, keepdims=True) * scale
            m = jnp.max(s, axis=0, keepdims=True)
            e = jnp.exp(s - m)
            p = e / jnp.sum(e, axis=0, keepdims=True)
            o_ref[b, h:h + 1, :] = jnp.sum(p * v_ref[b, :, h, :], axis=0, keepdims=True)


def _xattn_sample(q_s, kc, vc, layer, *, bb=4):
    cspec = pl.BlockSpec((None, bb, MEM_LEN, MEM_HEADS, MEM_HEAD_DIM), lambda i: (layer, i, 0, 0, 0))
    qspec = pl.BlockSpec((bb, MEM_HEADS, MEM_HEAD_DIM), lambda i: (i, 0, 0))
    return pl.pallas_call(
        functools.partial(_xattn_sample_kernel, bb=bb),
        grid=(DEC_BATCH // bb,),
        in_specs=[qspec, cspec, cspec],
        out_specs=qspec,
        out_shape=jax.ShapeDtypeStruct((DEC_BATCH, MEM_HEADS, MEM_HEAD_DIM), F32),
        compiler_params=_cparams("arbitrary"),
        name="xattn_sample",
    )(q_s, kc, vc)


def _mem_attn_ffn_block(hp, hs, hq, i, norm_ffn, xa_w_q, xa_w_o, k_p, v_p, kc, vc, w_gate, w_up, w_down):
    q_p, q_s = _matmul2(hq[0], hq[1], [xa_w_q], i, tn=1024, tm=1024, out_dtype=BF16)
    o_p = _xattn_prompt(q_p, k_p, v_p, i)
    o_s = _xattn_sample(q_s.astype(F32).reshape(DEC_BATCH, MEM_HEADS, MEM_HEAD_DIM), kc, vc, i)
    o_s = o_s.reshape(DEC_BATCH, D_MODEL).astype(BF16)
    hp, hs, hf_p, hf_s = _matmul_res_norm(o_p, o_s, xa_w_o, i, (hp, hs), norm_ffn[i])
    a_p, a_s = _matmul2(hf_p, hf_s, [w_gate, w_up], i, tn=512, tm=2048, out_dtype=BF16, epilogue=_swiglu_epilogue)
    return _matmul2(a_p, a_s, [w_down], i, tn=512, tm=512, res=(hp, hs))


def _short_conv_block(hp, hs, a, norm_g, norm_next, cache_sc, sc_w_in, sc_w_conv, sc_w_out):
    hn_p, hn_s = _rmsnorm2(hp, hs, norm_g, BF16)
    h0, h1 = cache_sc[a, :, 0, :], cache_sc[a, :, 1, :]
    y_p, y_s, u_s, hist_p = _sc_in_conv(hn_p, hn_s, sc_w_in, a, sc_w_conv[a], h0, h1)
    hp, hs, hq_p, hq_s = _matmul_res_norm(y_p, y_s, sc_w_out, a, (hp, hs), norm_next)
    return hp, hs, (hq_p, hq_s), hist_p, jnp.stack([h1, u_s], axis=1)


def _ssd_block(hp, hs, j, norm_g, conv_hist, state, ssd_w_in, conv_w, conv_b, dt_bias, a_log, d_skip, norm_ssd, w_out):
    hn_p, hn_s = _rmsnorm2(hp, hs, norm_g, BF16)
    w_in_t = jnp.swapaxes(ssd_w_in, 1, 2)
    zx_p, zx_s = _matmul2(hn_p, hn_s, [w_in_t], j, tn=1024, tm=1024, n_out=D_INNER + SSD_CONV_DIM,
                          w_transposed=True)

    dt_p, cs_p, dt_s, dec_s = _dt_prep(hn_p, hn_s, w_in_t, j, dt_bias[j], a_log[j])

    cw, cbias = conv_w[j], conv_b[j].reshape(1, SSD_CONV_DIM)
    ch = conv_hist[j]
    xbc_c_s, raw_s = _ssdconv_sample(zx_s, ch[:, 0, :], ch[:, 1, :], ch[:, 2, :], cw, cbias)
    chist_s = jnp.stack([ch[:, 1, :], ch[:, 2, :], raw_s], axis=1)

    dskip = jnp.repeat(d_skip[j], SSD_HEAD_DIM).reshape(SSD_GROUPS, 1, GROUP_W)
    ng = norm_ssd[j].reshape(SSD_GROUPS, 1, GROUP_W)

    g_p, sfin, chist_p = _ssd_scan_prompt(zx_p, cw, cbias, dt_p, cs_p, dt_p.T, cs_p.T, dskip, ng)
    ss_p = sfin.reshape(BATCH, SSD_HEADS, SSD_HEAD_DIM, SSD_STATE)

    def tile_major(v):
        return jnp.transpose(v.reshape(DEC_BATCH, N_TILES, LANES), (0, 2, 1))

    tiles_per_group = GROUP_W // LANES
    x_t = tile_major(xbc_c_s[:, :D_INNER])
    dt_t = tile_major(jnp.repeat(dt_s[:, :SSD_HEADS], SSD_HEAD_DIM, axis=1))
    b_g = xbc_c_s[:, D_INNER:D_INNER + BC_W].reshape(DEC_BATCH, SSD_GROUPS, SSD_STATE)
    c_g = xbc_c_s[:, D_INNER + BC_W:].reshape(DEC_BATCH, SSD_GROUPS, SSD_STATE)
    b_tiled = jnp.repeat(b_g, tiles_per_group, axis=1).reshape(DEC_BATCH, 1, D_INNER)
    c_exp = jnp.transpose(jnp.repeat(c_g, tiles_per_group, axis=1), (0, 2, 1))
    dec_b = jnp.broadcast_to(dec_s[:, :SSD_HEADS, None], (DEC_BATCH, SSD_HEADS, SSD_STATE))
    s_new, y_t = _ssd_step_sample(state[j].reshape(DEC_BATCH, D_INNER, SSD_STATE), x_t, dt_t, b_tiled, c_exp, dec_b)
    y_s = jnp.transpose(y_t, (0, 2, 1)).reshape(DEC_BATCH, D_INNER)
    g_s = _gate_norm_sample(y_s, xbc_c_s, zx_s, dskip, ng)
    ss_s = s_new.reshape(DEC_BATCH, SSD_HEADS, SSD_HEAD_DIM, SSD_STATE)

    hp, hs = _matmul2(g_p, g_s, [w_out], j, tn=512, tm=1024, res=(hp, hs))
    return hp, hs, chist_p, chist_s, ss_p, ss_s


def kernel(x_prompt, x_sample, mem_prompt, cache_sc, state_ssd_conv, state_ssd, cache_mem_k, cache_mem_v, norm_mix, norm_mem_q, norm_mem_kv, norm_ffn, norm_final, sc_w_in, sc_w_conv, sc_w_out, ssd_w_in, ssd_conv_w, ssd_conv_b, ssd_dt_bias, ssd_a_log, ssd_d, ssd_norm, ssd_w_out, xa_w_q, xa_w_k, xa_w_v, xa_w_o, ffn_w_gate, ffn_w_up, ffn_w_down):
    depth = norm_mix.shape[0]
    hp = x_prompt.reshape(T_P, D_MODEL)
    hs = x_sample.reshape(T_S, D_MODEL)

    mem = mem_prompt.reshape(BATCH * MEM_LEN, D_MODEL)
    k_rows, mem_k_p = _mem_proj(mem, norm_mem_kv, xa_w_k)
    v_rows, mem_v_p = _mem_proj(mem, norm_mem_kv, xa_w_v)

    sc_p, sc_s, sconv_p, sconv_s, ss_p, ss_s = [], [], [], [], [], []
    for i in range(depth):
        if i % 2 == 0:
            hp, hs, hq, a_p, a_s = _short_conv_block(hp, hs, i // 2, norm_mix[i], norm_mem_q[i], cache_sc,
                                                     sc_w_in, sc_w_conv, sc_w_out)
            sc_p.append(a_p)
            sc_s.append(a_s)
        else:
            hp, hs, c_p, c_s, s_p, s_s = _ssd_block(hp, hs, i // 2, norm_mix[i], state_ssd_conv, state_ssd, ssd_w_in,
                                                    ssd_conv_w, ssd_conv_b, ssd_dt_bias, ssd_a_log, ssd_d, ssd_norm,
                                                    ssd_w_out)
            sconv_p.append(c_p)
            sconv_s.append(c_s)
            ss_p.append(s_p)
            ss_s.append(s_s)
            hq = _rmsnorm2(hp, hs, norm_mem_q[i], BF16)
        hp, hs = _mem_attn_ffn_block(hp, hs, hq, i, norm_ffn, xa_w_q, xa_w_o, k_rows, v_rows,
                                     cache_mem_k, cache_mem_v, ffn_w_gate, ffn_w_up, ffn_w_down)

    y_p, y_s = _rmsnorm2(hp, hs, norm_final, F32)

    return (y_p.reshape(BATCH, SEQ, D_MODEL), y_s.reshape(DEC_BATCH, 1, D_MODEL),
            jnp.stack(sc_p), jnp.stack(sc_s), jnp.stack(sconv_p), jnp.stack(sconv_s),
            jnp.stack(ss_p), jnp.stack(ss_s), mem_k_p, mem_v_p)
```

```python
import functools

import jax
import jax.numpy as jnp
from jax import lax
from jax.experimental import pallas as pl
from jax.experimental.pallas import tpu as pltpu

F32 = jnp.float32
BF16 = jnp.bfloat16

D_MODEL = 2048
BATCH = 4
SEQ = 2048
DEC_BATCH = 128
T_P = BATCH * SEQ
T_S = DEC_BATCH
D_INNER = 4096
SSD_HEAD_DIM = 64
SSD_HEADS = 64
SSD_GROUPS = 8
SSD_HPG = 8
SSD_STATE = 128
GROUP_W = SSD_HPG * SSD_HEAD_DIM
BC_W = SSD_GROUPS * SSD_STATE
SSD_CONV_DIM = D_INNER + 2 * BC_W
MEM_LEN = 256
MEM_HEADS = 4
MEM_HEAD_DIM = 512
RMS_EPS = 1e-5
LANES = 128

V7X_VMEM_LIMIT_BYTES = 56 * 1024 * 1024
SSD_CHUNK = 128
SSD_STEP_SEQS = 2
N_TILES = D_INNER // LANES


def _cparams(*sem):
    return pltpu.CompilerParams(dimension_semantics=sem, vmem_limit_bytes=V7X_VMEM_LIMIT_BYTES)


def _silu(x):
    hx = 0.5 * x
    return hx * (1.0 + jnp.tanh(hx))


def _rms(x, g):
    ms = jnp.mean(x * x, axis=-1, keepdims=True)
    return x * lax.rsqrt(ms + RMS_EPS) * g


def _rmsnorm_kernel(xp_ref, xs_ref, g_ref, op_ref, os_ref, *, n_p):
    i = pl.program_id(0)

    @pl.when(i < n_p)
    def _():
        op_ref[...] = _rms(xp_ref[...], g_ref[...]).astype(op_ref.dtype)

    @pl.when(i == n_p)
    def _():
        os_ref[...] = _rms(xs_ref[...], g_ref[...]).astype(os_ref.dtype)


def _rmsnorm2(xp, xs, g, out_dtype, *, tm=1024):
    d = xp.shape[1]
    n_p = xp.shape[0] // tm
    pspec = pl.BlockSpec((tm, d), lambda i: (jnp.minimum(i, n_p - 1), 0))
    sspec = pl.BlockSpec((T_S, d), lambda i: (0, 0))
    return pl.pallas_call(
        functools.partial(_rmsnorm_kernel, n_p=n_p),
        grid=(n_p + 1,),
        in_specs=[pspec, sspec, pl.BlockSpec((1, d), lambda i: (0, 0))],
        out_specs=[pspec, sspec],
        out_shape=[jax.ShapeDtypeStruct(xp.shape, out_dtype), jax.ShapeDtypeStruct(xs.shape, out_dtype)],
        compiler_params=_cparams("arbitrary"),
        name="rmsnorm",
    )(xp, xs, g.reshape(1, d))


def _mem_proj_kernel(x_ref, g_ref, w_ref, o2_ref, o5_ref, wb_ref, *, seqs):
    @pl.when(pl.program_id(1) == 0)
    def _():
        wb_ref[...] = w_ref[...].astype(BF16)

    m = _rms(x_ref[...], g_ref[...]).astype(BF16)
    acc = jnp.dot(m, wb_ref[...], preferred_element_type=F32)
    o2_ref[...] = acc
    for s in range(seqs):
        for h in range(MEM_HEADS):
            o5_ref[s, :, h, :] = acc[MEM_LEN * s:MEM_LEN * (s + 1), MEM_HEAD_DIM * h:MEM_HEAD_DIM * (h + 1)]


def _mem_proj(mem, norm_g, w3, *, seqs=2):
    depth = w3.shape[0]
    rows, d = mem.shape
    tm = seqs * MEM_LEN
    return pl.pallas_call(
        functools.partial(_mem_proj_kernel, seqs=seqs),
        grid=(depth, rows // tm),
        in_specs=[pl.BlockSpec((tm, d), lambda l, i: (i, 0)),
                  pl.BlockSpec((None, 1, d), lambda l, i: (l, 0, 0)),
                  pl.BlockSpec((None, d, d), lambda l, i: (l, 0, 0), pipeline_mode=pl.Buffered(1))],
        out_specs=[pl.BlockSpec((None, tm, d), lambda l, i: (l, i, 0)),
                   pl.BlockSpec((None, seqs, MEM_LEN, MEM_HEADS, MEM_HEAD_DIM), lambda l, i: (l, i, 0, 0, 0))],
        out_shape=[jax.ShapeDtypeStruct((depth, rows, d), F32),
                   jax.ShapeDtypeStruct((depth, rows // MEM_LEN, MEM_LEN, MEM_HEADS, MEM_HEAD_DIM), F32)],
        scratch_shapes=[pltpu.VMEM((d, d), BF16)],
        compiler_params=_cparams("arbitrary", "arbitrary"),
        name="mem_proj",
    )(mem, norm_g.reshape(depth, 1, d), w3)


_NT_DIMS = (((1,), (1,)), ((), ()))


def _mm_kernel(*refs, has_s, has_res, n_w, epilogue, w_transposed):
    it = iter(refs)
    xp_ref = next(it)
    xs_ref = next(it) if has_s else None
    w_refs = [next(it) for _ in range(n_w)]
    rp_ref = next(it) if has_res else None
    rs_ref = next(it) if (has_res and has_s) else None
    op_ref = next(it)
    os_ref = next(it) if has_s else None
    wb_refs = [next(it) for _ in range(n_w)]

    def run(x_ref, r_ref, o_ref):
        x = x_ref[...]
        if w_transposed:
            accs = [lax.dot_general(x, wb_ref[...], _NT_DIMS, preferred_element_type=F32) for wb_ref in wb_refs]
        else:
            accs = [jnp.dot(x, wb_ref[...], preferred_element_type=F32) for wb_ref in wb_refs]
        out = epilogue(*accs)
        if r_ref is not None:
            out = out + r_ref[...]
        o_ref[...] = out.astype(o_ref.dtype)

    @pl.when(pl.program_id(1) == 0)
    def _():
        for w_ref, wb_ref in zip(w_refs, wb_refs):
            wb_ref[...] = w_ref[...].astype(BF16)
        if has_s:
            run(xs_ref, rs_ref, os_ref)

    run(xp_ref, rp_ref, op_ref)


def _identity(a):
    return a


def _swiglu_epilogue(a, u):
    return _silu(a) * u


def _matmul2(xp, xs, ws, layer, *, tn, tm, n_out=None, res=None, out_dtype=F32, epilogue=_identity,
             w_transposed=False):
    m, k = xp.shape
    n = n_out if n_out is not None else ws[0].shape[1 if w_transposed else 2]
    has_s = xs is not None
    xp_spec = pl.BlockSpec((tm, k), lambda j, i: (i, 0))
    xs_spec = pl.BlockSpec((T_S, k), lambda j, i: (0, 0))
    if w_transposed:
        w_spec = pl.BlockSpec((None, tn, k), lambda j, i: (layer, j, 0))
    else:
        w_spec = pl.BlockSpec((None, k, tn), lambda j, i: (layer, 0, j))
    op_spec = pl.BlockSpec((tm, tn), lambda j, i: (i, j))
    os_spec = pl.BlockSpec((T_S, tn), lambda j, i: (0, j))

    in_specs, args = [xp_spec], [xp]
    if has_s:
        in_specs.append(xs_spec)
        args.append(xs)
    in_specs += [w_spec] * len(ws)
    args += list(ws)
    if res is not None:
        in_specs.append(op_spec)
        args.append(res[0])
        if has_s:
            in_specs.append(os_spec)
            args.append(res[1])
    out_specs = [op_spec] + ([os_spec] if has_s else [])
    out_shape = [jax.ShapeDtypeStruct((m, n), out_dtype)] + ([jax.ShapeDtypeStruct((T_S, n), out_dtype)] if has_s else [])
    outs = pl.pallas_call(
        functools.partial(_mm_kernel, has_s=has_s, has_res=res is not None, n_w=len(ws), epilogue=epilogue,
                          w_transposed=w_transposed),
        grid=(n // tn, m // tm),
        in_specs=in_specs,
        out_specs=out_specs,
        out_shape=out_shape,
        scratch_shapes=[pltpu.VMEM((tn, k) if w_transposed else (k, tn), BF16) for _ in ws],
        compiler_params=_cparams("arbitrary", "arbitrary"),
        name="matmul",
    )(*args)
    return tuple(outs) if has_s else outs[0]


def _mm_res_norm_kernel(xp_ref, xs_ref, w_ref, rp_ref, rs_ref, g_ref, hp_ref, hs_ref, np_ref, ns_ref, wb_ref):
    def run(x_ref, r_ref, h_ref, n_ref):
        h = jnp.dot(x_ref[...], wb_ref[...], preferred_element_type=F32) + r_ref[...]
        h_ref[...] = h
        n_ref[...] = _rms(h, g_ref[...]).astype(n_ref.dtype)

    @pl.when(pl.program_id(0) == 0)
    def _():
        wb_ref[...] = w_ref[...].astype(BF16)
        run(xs_ref, rs_ref, hs_ref, ns_ref)

    run(xp_ref, rp_ref, hp_ref, np_ref)


def _matmul_res_norm(xp, xs, w3, layer, res, g, *, tm=512):
    m, k = xp.shape
    n = w3.shape[2]
    pspec = pl.BlockSpec((tm, n), lambda i: (i, 0))
    sspec = pl.BlockSpec((T_S, n), lambda i: (0, 0))
    wspec = pl.BlockSpec((None, k, n), lambda i: (layer, 0, 0), pipeline_mode=pl.Buffered(1))
    return pl.pallas_call(
        _mm_res_norm_kernel,
        grid=(m // tm,),
        in_specs=[pl.BlockSpec((tm, k), lambda i: (i, 0)), pl.BlockSpec((T_S, k), lambda i: (0, 0)),
                  wspec, pspec, sspec, pl.BlockSpec((1, n), lambda i: (0, 0))],
        out_specs=[pspec, sspec, pspec, sspec],
        out_shape=[jax.ShapeDtypeStruct((m, n), F32), jax.ShapeDtypeStruct((T_S, n), F32),
                   jax.ShapeDtypeStruct((m, n), BF16), jax.ShapeDtypeStruct((T_S, n), BF16)],
        scratch_shapes=[pltpu.VMEM((k, n), BF16)],
        compiler_params=_cparams("arbitrary"),
        name="matmul_res_norm",
    )(xp, xs, w3, res[0], res[1], g.reshape(1, n))


def _shifted_rows(u, prev8, k):
    row8 = lax.broadcasted_iota(jnp.int32, prev8.shape, 0)
    r = pltpu.roll(u, k, axis=0)
    top = jnp.where(row8 < k, pltpu.roll(prev8, k, axis=0), r[0:8, :])
    return jnp.concatenate([top, r[8:, :]], axis=0)


def _sc_in_kernel(xp_ref, xs_ref, wb_ref, wc_ref, wv_ref, cw_ref, h0_ref, h1_ref,
                  yp_ref, ys_ref, us_ref, hist_ref, wbb_ref, wcb_ref, wvb_ref, prev_ref, *, tiles_per_seq):
    i = pl.program_id(1)
    cw = cw_ref[...]

    def gates(x):
        return (jnp.dot(x, wbb_ref[...], preferred_element_type=F32),
                jnp.dot(x, wcb_ref[...], preferred_element_type=F32),
                jnp.dot(x, wvb_ref[...], preferred_element_type=F32))

    @pl.when(i == 0)
    def _():
        wbb_ref[...] = wb_ref[...].astype(BF16)
        wcb_ref[...] = wc_ref[...].astype(BF16)
        wvb_ref[...] = wv_ref[...].astype(BF16)
        bg, cg, v = gates(xs_ref[...])
        u = cg * v
        conv = cw[2:3, :] * u + cw[1:2, :] * h1_ref[...] + cw[0:1, :] * h0_ref[...]
        ys_ref[...] = (bg * conv).astype(ys_ref.dtype)
        us_ref[...] = u

    @pl.when(i % tiles_per_seq == 0)
    def _():
        prev_ref[...] = jnp.zeros_like(prev_ref)

    bg, cg, v = gates(xp_ref[...])
    u = cg * v
    prev8 = prev_ref[...]
    conv = cw[2:3, :] * u + cw[1:2, :] * _shifted_rows(u, prev8, 1) + cw[0:1, :] * _shifted_rows(u, prev8, 2)
    yp_ref[...] = (bg * conv).astype(yp_ref.dtype)
    tm = u.shape[0]
    prev_ref[...] = u[tm - 8:, :]

    @pl.when(i % tiles_per_seq == tiles_per_seq - 1)
    def _():
        hist_ref[...] = u[tm - 2:, :]


def _sc_in_conv(xp, xs, w_in3, layer, w_conv, h0, h1, *, tn=512, tm=1024):
    k = xp.shape[1]
    nb = D_MODEL // tn
    tiles_per_seq = SEQ // tm

    def wspec(part):
        return pl.BlockSpec((None, k, tn), lambda j, i: (layer, 0, j + part * nb))

    sspec = pl.BlockSpec((T_S, tn), lambda j, i: (0, j))
    pspec = pl.BlockSpec((tm, tn), lambda j, i: (i, j))
    return pl.pallas_call(
        functools.partial(_sc_in_kernel, tiles_per_seq=tiles_per_seq),
        grid=(nb, T_P // tm),
        in_specs=[pl.BlockSpec((tm, k), lambda j, i: (i, 0)),
                  pl.BlockSpec((T_S, k), lambda j, i: (0, 0)),
                  wspec(0), wspec(1), wspec(2),
                  pl.BlockSpec((3, tn), lambda j, i: (0, j)), sspec, sspec],
        out_specs=[pspec, sspec, sspec,
                   pl.BlockSpec((None, 2, tn), lambda j, i: (i // tiles_per_seq, 0, j))],
        out_shape=[jax.ShapeDtypeStruct((T_P, D_MODEL), BF16),
                   jax.ShapeDtypeStruct((T_S, D_MODEL), BF16),
                   jax.ShapeDtypeStruct((T_S, D_MODEL), F32),
                   jax.ShapeDtypeStruct((BATCH, 2, D_MODEL), F32)],
        scratch_shapes=[pltpu.VMEM((k, tn), BF16), pltpu.VMEM((k, tn), BF16), pltpu.VMEM((k, tn), BF16),
                        pltpu.VMEM((8, tn), F32)],
        compiler_params=_cparams("arbitrary", "arbitrary"),
        name="sc_in_conv",
    )(xp, xs, w_in3, w_in3, w_in3, w_conv, h0, h1)


def _ssdconv_sample_kernel(x_ref, h0_ref, h1_ref, h2_ref, w_ref, b_ref, o_ref, raw_ref):
    x = x_ref[...]
    w = w_ref[...]
    conv = w[3:4, :] * x + w[2:3, :] * h2_ref[...] + w[1:2, :] * h1_ref[...] + w[0:1, :] * h0_ref[...]
    o_ref[...] = _silu(conv + b_ref[...])
    raw_ref[...] = x


def _ssdconv_sample(zx, h0, h1, h2, conv_w, conv_b, *, tc=512):
    nb = SSD_CONV_DIM // tc
    off = D_INNER // tc
    hspec = pl.BlockSpec((T_S, tc), lambda j: (0, j))
    return pl.pallas_call(
        _ssdconv_sample_kernel,
        grid=(nb,),
        in_specs=[pl.BlockSpec((T_S, tc), lambda j: (0, j + off)), hspec, hspec, hspec,
                  pl.BlockSpec((4, tc), lambda j: (0, j)),
                  pl.BlockSpec((1, tc), lambda j: (0, j))],
        out_specs=[hspec, hspec],
        out_shape=[jax.ShapeDtypeStruct((T_S, SSD_CONV_DIM), F32),
                   jax.ShapeDtypeStruct((T_S, SSD_CONV_DIM), F32)],
        compiler_params=_cparams("arbitrary"),
        name="ssdconv_sample",
    )(zx, h0, h1, h2, conv_w, conv_b)


def _dt_kernel(xp_ref, xs_ref, w_ref, bias_ref, alog_ref, dtp_ref, csp_ref, dts_ref, decs_ref, *, n_p):
    i = pl.program_id(0)

    def dt_da(x_ref):
        w = w_ref[:SSD_HEADS, :].astype(BF16)
        raw = lax.dot_general(x_ref[...], w, _NT_DIMS, preferred_element_type=F32) + bias_ref[...]
        dt = jnp.maximum(raw, 0.0) + jnp.log1p(jnp.exp(-jnp.abs(raw)))
        return dt, dt * (-jnp.exp(alog_ref[...]))

    @pl.when(i < n_p)
    def _():
        dt, da = dt_da(xp_ref)
        dtp_ref[...] = dt
        row_in_chunk = lax.broadcasted_iota(jnp.int32, da.shape, 0) % SSD_CHUNK
        cs = da
        k = 1
        while k < SSD_CHUNK:
            cs = cs + jnp.where(row_in_chunk >= k, pltpu.roll(cs, k, axis=0), 0.0)
            k *= 2
        csp_ref[...] = cs

    @pl.when(i == n_p)
    def _():
        dt, da = dt_da(xs_ref)
        dts_ref[...] = dt
        decs_ref[...] = jnp.exp(da)


def _dt_prep(hn_p, hn_s, w_in3t, layer, dt_bias, a_log):
    k = hn_p.shape[1]
    tm = 8 * SSD_CHUNK
    n_p = T_P // tm
    dt_block = (D_INNER + SSD_CONV_DIM) // LANES
    pin = pl.BlockSpec((tm, k), lambda i: (jnp.minimum(i, n_p - 1), 0))
    pout = pl.BlockSpec((tm, SSD_HEADS), lambda i: (jnp.minimum(i, n_p - 1), 0))
    sout = pl.BlockSpec((T_S, SSD_HEADS), lambda i: (0, 0))
    one = pl.BlockSpec((1, SSD_HEADS), lambda i: (0, 0))
    return pl.pallas_call(
        functools.partial(_dt_kernel, n_p=n_p),
        grid=(n_p + 1,),
        in_specs=[pin, pl.BlockSpec((T_S, k), lambda i: (0, 0)),
                  pl.BlockSpec((None, LANES, k), lambda i: (layer, dt_block, 0)), one, one],
        out_specs=[pout, pout, sout, sout],
        out_shape=[jax.ShapeDtypeStruct((T_P, SSD_HEADS), F32), jax.ShapeDtypeStruct((T_P, SSD_HEADS), F32),
                   jax.ShapeDtypeStruct((T_S, SSD_HEADS), F32), jax.ShapeDtypeStruct((T_S, SSD_HEADS), F32)],
        compiler_params=_cparams("arbitrary"),
        name="ssd_dt",
    )(hn_p, hn_s, w_in3t, dt_bias.reshape(1, SSD_HEADS), a_log.reshape(1, SSD_HEADS))


def _expand_heads(v, onehot):
    hi = v.astype(BF16)
    lo = (v - hi.astype(F32)).astype(BF16)
    return (jnp.dot(hi, onehot, preferred_element_type=F32)
            + jnp.dot(lo, onehot, preferred_element_type=F32))


def _gate_norm(y, z, ng):
    return _rms(y * _silu(z), ng)


def _ssd_group_chunk(x, bm, cm, z, dtc, csc, dtr, csr, dsk, ng, st):
    q = SSD_CHUNK
    xb = x.astype(BF16)
    bb = bm.astype(BF16)
    cb = cm.astype(BF16)

    lane = lax.broadcasted_iota(jnp.int32, (SSD_HPG, GROUP_W), 1)
    head = lax.broadcasted_iota(jnp.int32, (SSD_HPG, GROUP_W), 0)
    onehot = jnp.where(lane // SSD_HEAD_DIM == head, 1.0, 0.0).astype(BF16)

    cbm = lax.dot_general(cb, bb, (((1,), (1,)), ((), ())), preferred_element_type=F32)
    row = lax.broadcasted_iota(jnp.int32, (q, q), 0)
    col = lax.broadcasted_iota(jnp.int32, (q, q), 1)
    tri = row >= col
    lane128 = lax.broadcasted_iota(jnp.int32, (q, LANES), 1)
    lo_half = lane128 < SSD_HEAD_DIM

    tiles = []
    for m in range(GROUP_W // LANES):
        xt = xb[:, LANES * m:LANES * (m + 1)]
        acc = None
        for j, xm in ((2 * m, jnp.where(lo_half, xt, 0)), (2 * m + 1, jnp.where(lo_half, 0, xt))):
            seg = csc[:, j:j + 1] - csr[j:j + 1, :]
            dec = jnp.exp(jnp.where(tri, seg, -jnp.inf))
            w = (cbm * dec * dtr[j:j + 1, :]).astype(BF16)
            part = jnp.dot(w, xm, preferred_element_type=F32)
            acc = part if acc is None else acc + part
        tiles.append(acc)
    y_diag = jnp.concatenate(tiles, axis=1)

    e_exp = _expand_heads(jnp.exp(csc), onehot)
    y_off = jnp.dot(cb, st.astype(BF16), preferred_element_type=F32) * e_exp

    cs_last = csc[q - 1:q, :]
    te = _expand_heads(jnp.exp(cs_last - csc) * dtc, onehot)
    xw = (x * te).astype(BF16)
    d_st = jnp.dot(bm.T.astype(BF16), xw, preferred_element_type=F32)
    st_new = st * e_exp[q - 1:q, :] + d_st
    y = y_diag + y_off + dsk * x
    return _gate_norm(y, z, ng), st_new


def _causal_conv4_silu(x, prev8, w, bias):
    acc = w[3:4, :] * x + bias
    for k in (1, 2, 3):
        acc = acc + w[3 - k:4 - k, :] * _shifted_rows(x, prev8, k)
    return _silu(acc)


def _ssd_mix_kernel(x_ref, b_ref, c_ref, z_ref, wx_ref, wb_ref, wc_ref, bx_ref, bb_ref, bc_ref,
                    dtc_ref, csc_ref, dtr_ref, csr_ref, dsk_ref, ng_ref,
                    s_ref, xt_ref, dtt_ref, bt_ref, ce_ref, dec_ref,
                    y_ref, sfin_ref, hist_ref, so_ref, yt_ref,
                    st_ref, px_ref, pb_ref, pc_ref):
    q = SSD_CHUNK
    c = pl.program_id(1)

    @pl.when(c == 0)
    def _():
        st_ref[...] = jnp.zeros_like(st_ref)
        px_ref[...] = jnp.zeros_like(px_ref)
        pb_ref[...] = jnp.zeros_like(pb_ref)
        pc_ref[...] = jnp.zeros_like(pc_ref)

    _ssd_step_body(s_ref, xt_ref, dtt_ref, bt_ref, ce_ref, dec_ref, so_ref, yt_ref)

    for gi in range(SSD_GROUPS):
        xs = slice(GROUP_W * gi, GROUP_W * (gi + 1))
        ns = slice(SSD_STATE * gi, SSD_STATE * (gi + 1))
        hs = slice(SSD_HPG * gi, SSD_HPG * (gi + 1))
        x = _causal_conv4_silu(x_ref[:, xs], px_ref[:, xs], wx_ref[:, xs], bx_ref[:, xs])
        bm = _causal_conv4_silu(b_ref[:, ns], pb_ref[:, ns], wb_ref[:, ns], bb_ref[:, ns])
        cm = _causal_conv4_silu(c_ref[:, ns], pc_ref[:, ns], wc_ref[:, ns], bc_ref[:, ns])
        y, st_new = _ssd_group_chunk(x, bm, cm, z_ref[:, xs],
                                     dtc_ref[:, hs], csc_ref[:, hs], dtr_ref[hs, :], csr_ref[hs, :],
                                     dsk_ref[gi], ng_ref[gi], st_ref[gi])
        y_ref[:, xs] = y.astype(y_ref.dtype)
        st_ref[gi] = st_new

    px_ref[...] = x_ref[q - 8:, :]
    pb_ref[...] = b_ref[q - 8:, :]
    pc_ref[...] = c_ref[q - 8:, :]

    @pl.when(c == pl.num_programs(1) - 1)
    def _():
        for gi in range(SSD_GROUPS):
            sfin_ref[GROUP_W * gi:GROUP_W * (gi + 1), :] = st_ref[gi].T
        hist_ref[:, 0:D_INNER] = x_ref[q - 3:, :]
        hist_ref[:, D_INNER:D_INNER + BC_W] = b_ref[q - 3:, :]
        hist_ref[:, D_INNER + BC_W:] = c_ref[q - 3:, :]


def _ssd_mix(zx, conv_w, conv_b, dtc, csc, dtr, csr, dskip, ng, state, x_t, dt_t, b_tiled, c_exp, dec):
    q = SSD_CHUNK
    nc = SEQ // q
    assert BATCH * nc * SSD_STEP_SEQS == DEC_BATCH
    b_blk = D_INNER // BC_W
    small_c = pl.BlockSpec((q, SSD_HEADS), lambda b, c: (b * nc + c, 0))
    small_r = pl.BlockSpec((SSD_HEADS, q), lambda b, c: (0, b * nc + c))
    pspec = pl.BlockSpec((SSD_GROUPS, 1, GROUP_W), lambda b, c: (0, 0, 0))

    def rows(width, col_block):
        return pl.BlockSpec((q, width), lambda b, c: (b * nc + c, col_block))

    def par(nrows, width, col_block):
        return pl.BlockSpec((nrows, width), lambda b, c: (0, col_block))

    def seqs(*dims):
        return pl.BlockSpec((SSD_STEP_SEQS,) + dims, lambda b, c: (b * nc + c, 0, 0))

    tspec = seqs(LANES, N_TILES)
    sspec = seqs(D_INNER, SSD_STATE)
    return pl.pallas_call(
        _ssd_mix_kernel,
        grid=(BATCH, nc),
        in_specs=[rows(D_INNER, 1), rows(BC_W, 2 * b_blk), rows(BC_W, 2 * b_blk + 1), rows(D_INNER, 0),
                  par(4, D_INNER, 0), par(4, BC_W, b_blk), par(4, BC_W, b_blk + 1),
                  par(1, D_INNER, 0), par(1, BC_W, b_blk), par(1, BC_W, b_blk + 1),
                  small_c, small_c, small_r, small_r, pspec, pspec,
                  sspec, tspec, tspec, seqs(1, D_INNER), seqs(SSD_STATE, N_TILES), seqs(SSD_HEADS, SSD_STATE)],
        out_specs=[rows(D_INNER, 0),
                   pl.BlockSpec((D_INNER, SSD_STATE), lambda b, c: (b, 0)),
                   pl.BlockSpec((None, 3, SSD_CONV_DIM), lambda b, c: (b, 0, 0)),
                   sspec, tspec],
        out_shape=[jax.ShapeDtypeStruct((T_P, D_INNER), BF16),
                   jax.ShapeDtypeStruct((BATCH * D_INNER, SSD_STATE), F32),
                   jax.ShapeDtypeStruct((BATCH, 3, SSD_CONV_DIM), F32),
                   jax.ShapeDtypeStruct((DEC_BATCH, D_INNER, SSD_STATE), F32),
                   jax.ShapeDtypeStruct((DEC_BATCH, LANES, N_TILES), F32)],
        scratch_shapes=[pltpu.VMEM((SSD_GROUPS, SSD_STATE, GROUP_W), F32),
                        pltpu.VMEM((8, D_INNER), F32), pltpu.VMEM((8, BC_W), F32), pltpu.VMEM((8, BC_W), F32)],
        compiler_params=_cparams("arbitrary", "arbitrary"),
        name="ssd_mix",
    )(zx, zx, zx, zx, conv_w, conv_w, conv_w, conv_b, conv_b, conv_b, dtc, csc, dtr, csr, dskip, ng,
      state, x_t, dt_t, b_tiled, c_exp, dec)


def _ssd_step_body(s_ref, xt_ref, dtt_ref, bt_ref, ce_ref, dec_ref, so_ref, yt_ref):
    tile_of_lane = lax.broadcasted_iota(jnp.int32, (N_TILES, D_INNER), 1) // LANES
    row32 = lax.broadcasted_iota(jnp.int32, (N_TILES, D_INNER), 0)
    lane32 = lax.broadcasted_iota(jnp.int32, (SSD_STATE, N_TILES), 1)
    for b in range(SSD_STEP_SEQS):
        xdt = (xt_ref[b] * dtt_ref[b]).astype(BF16)
        b_big = jnp.where(tile_of_lane == row32, bt_ref[b], 0.0).astype(BF16)
        upd = jnp.dot(xdt, b_big, preferred_element_type=F32)

        ce = ce_ref[b]
        s_tiles, c_tiles = [], []
        for r in range(N_TILES):
            dtile = jnp.concatenate(
                [jnp.broadcast_to(dec_ref[b, 2 * r:2 * r + 1, :], (SSD_HEAD_DIM, SSD_STATE)),
                 jnp.broadcast_to(dec_ref[b, 2 * r + 1:2 * r + 2, :], (SSD_HEAD_DIM, SSD_STATE))], axis=0)
            s_n = s_ref[b, LANES * r:LANES * (r + 1), :] * dtile + upd[:, LANES * r:LANES * (r + 1)]
            so_ref[b, LANES * r:LANES * (r + 1), :] = s_n
            s_tiles.append(s_n.astype(BF16))
            c_tiles.append(jnp.where(lane32 == r, ce, 0.0).astype(BF16))
        s_big = jnp.concatenate(s_tiles, axis=1)
        c_big = jnp.concatenate(c_tiles, axis=0)
        yt_ref[b] = jnp.dot(s_big, c_big, preferred_element_type=F32)


def _gate_norm_kernel(y_ref, x_ref, z_ref, dsk_ref, ng_ref, o_ref):
    y = y_ref[...] + dsk_ref[...] * x_ref[...]
    o_ref[...] = _gate_norm(y, z_ref[...], ng_ref[...]).astype(o_ref.dtype)


def _gate_norm_sample(y_s, xbc_c_s, zx_s, dskip, ng):
    gspec = pl.BlockSpec((T_S, GROUP_W), lambda g: (0, g))
    pspec = pl.BlockSpec((None, 1, GROUP_W), lambda g: (g, 0, 0))
    return pl.pallas_call(
        _gate_norm_kernel,
        grid=(SSD_GROUPS,),
        in_specs=[gspec, gspec, gspec, pspec, pspec],
        out_specs=gspec,
        out_shape=jax.ShapeDtypeStruct((T_S, D_INNER), BF16),
        compiler_params=_cparams("arbitrary"),
        name="ssd_gate_norm",
    )(y_s, xbc_c_s, zx_s, dskip, ng)


def _xattn_prompt_kernel(q_ref, k_ref, v_ref, o_ref):
    scale = MEM_HEAD_DIM ** -0.5
    for h in range(MEM_HEADS):
        sl = slice(MEM_HEAD_DIM * h, MEM_HEAD_DIM * (h + 1))
        kh = k_ref[:, sl].astype(BF16)
        vh = v_ref[:, sl].astype(BF16)
        s = lax.dot_general(q_ref[:, sl], kh, (((1,), (1,)), ((), ())), preferred_element_type=F32) * scale
        m = jnp.max(s, axis=-1, keepdims=True)
        e = jnp.exp(s - m)
        p = (e / jnp.sum(e, axis=-1, keepdims=True)).astype(BF16)
        o_ref[:, sl] = jnp.dot(p, vh, preferred_element_type=F32).astype(o_ref.dtype)


def _xattn_prompt(q, k, v, layer, *, tq=1024):
    nq = SEQ // tq
    kspec = pl.BlockSpec((None, MEM_LEN, D_MODEL), lambda b, i: (layer, b, 0))
    return pl.pallas_call(
        _xattn_prompt_kernel,
        grid=(BATCH, nq),
        in_specs=[pl.BlockSpec((tq, D_MODEL), lambda b, i: (b * nq + i, 0)), kspec, kspec],
        out_specs=pl.BlockSpec((tq, D_MODEL), lambda b, i: (b * nq + i, 0)),
        out_shape=jax.ShapeDtypeStruct((T_P, D_MODEL), BF16),
        compiler_params=_cparams("arbitrary", "arbitrary"),
        name="xattn_prompt",
    )(q, k, v)


def _xattn_sample_kernel(q_ref, k_ref, v_ref, o_ref, *, bb):
    scale = MEM_HEAD_DIM ** -0.5
    for b in range(bb):
        for h in range(MEM_HEADS):
            kh = k_ref[b, :, h, :]
            s = jnp.sum(kh * q_ref[b, h:h + 1, :], axis=-1, keepdims=True) * scale
            m = jnp.max(s, axis=0, keepdims=True)
            e = jnp.exp(s - m)
            p = e / jnp.sum(e, axis=0, keepdims=True)
            o_ref[b, h:h + 1, :] = jnp.sum(p * v_ref[b, :, h, :], axis=0, keepdims=True)


def _xattn_sample(q_s, kc, vc, layer, *, bb=4):
    cspec = pl.BlockSpec((None, bb, MEM_LEN, MEM_HEADS, MEM_HEAD_DIM), lambda i: (layer, i, 0, 0, 0))
    qspec = pl.BlockSpec((bb, MEM_HEADS, MEM_HEAD_DIM), lambda i: (i, 0, 0))
    return pl.pallas_call(
        functools.partial(_xattn_sample_kernel, bb=bb),
        grid=(DEC_BATCH // bb,),
        in_specs=[qspec, cspec, cspec],
        out_specs=qspec,
        out_shape=jax.ShapeDtypeStruct((DEC_BATCH, MEM_HEADS, MEM_HEAD_DIM), F32),
        compiler_params=_cparams("arbitrary"),
        name="xattn_sample",
    )(q_s, kc, vc)


def _mem_attn_ffn_block(hp, hs, hq, i, norm_ffn, xa_w_q, xa_w_o, k_p, v_p, kc, vc, w_gate, w_up, w_down):
    q_p, q_s = _matmul2(hq[0], hq[1], [xa_w_q], i, tn=1024, tm=1024, out_dtype=BF16)
    o_p = _xattn_prompt(q_p, k_p, v_p, i)
    o_s = _xattn_sample(q_s.astype(F32).reshape(DEC_BATCH, MEM_HEADS, MEM_HEAD_DIM), kc, vc, i)
    o_s = o_s.reshape(DEC_BATCH, D_MODEL).astype(BF16)
    hp, hs, hf_p, hf_s = _matmul_res_norm(o_p, o_s, xa_w_o, i, (hp, hs), norm_ffn[i])
    a_p, a_s = _matmul2(hf_p, hf_s, [w_gate, w_up], i, tn=512, tm=1024, out_dtype=BF16, epilogue=_swiglu_epilogue)
    return _matmul2(a_p, a_s, [w_down], i, tn=512, tm=512, res=(hp, hs))


def _short_conv_block(hp, hs, a, norm_g, norm_next, cache_sc, sc_w_in, sc_w_conv, sc_w_out):
    hn_p, hn_s = _rmsnorm2(hp, hs, norm_g, BF16)
    h0, h1 = cache_sc[a, :, 0, :], cache_sc[a, :, 1, :]
    y_p, y_s, u_s, hist_p = _sc_in_conv(hn_p, hn_s, sc_w_in, a, sc_w_conv[a], h0, h1)
    hp, hs, hq_p, hq_s = _matmul_res_norm(y_p, y_s, sc_w_out, a, (hp, hs), norm_next)
    return hp, hs, (hq_p, hq_s), hist_p, jnp.stack([h1, u_s], axis=1)


def _ssd_block(hp, hs, j, norm_g, conv_hist, state, ssd_w_in, conv_w, conv_b, dt_bias, a_log, d_skip, norm_ssd, w_out):
    hn_p, hn_s = _rmsnorm2(hp, hs, norm_g, BF16)
    w_in_t = jnp.swapaxes(ssd_w_in, 1, 2)
    zx_p, zx_s = _matmul2(hn_p, hn_s, [w_in_t], j, tn=1024, tm=1024, n_out=D_INNER + SSD_CONV_DIM,
                          w_transposed=True)

    dt_p, cs_p, dt_s, dec_s = _dt_prep(hn_p, hn_s, w_in_t, j, dt_bias[j], a_log[j])

    cw, cbias = conv_w[j], conv_b[j].reshape(1, SSD_CONV_DIM)
    ch = conv_hist[j]
    xbc_c_s, raw_s = _ssdconv_sample(zx_s, ch[:, 0, :], ch[:, 1, :], ch[:, 2, :], cw, cbias)
    chist_s = jnp.stack([ch[:, 1, :], ch[:, 2, :], raw_s], axis=1)

    dskip = jnp.repeat(d_skip[j], SSD_HEAD_DIM).reshape(SSD_GROUPS, 1, GROUP_W)
    ng = norm_ssd[j].reshape(SSD_GROUPS, 1, GROUP_W)

    def tile_major(v):
        return jnp.transpose(v.reshape(DEC_BATCH, N_TILES, LANES), (0, 2, 1))

    tiles_per_group = GROUP_W // LANES
    x_t = tile_major(xbc_c_s[:, :D_INNER])
    dt_t = tile_major(jnp.repeat(dt_s[:, :SSD_HEADS], SSD_HEAD_DIM, axis=1))
    b_g = xbc_c_s[:, D_INNER:D_INNER + BC_W].reshape(DEC_BATCH, SSD_GROUPS, SSD_STATE)
    c_g = xbc_c_s[:, D_INNER + BC_W:].reshape(DEC_BATCH, SSD_GROUPS, SSD_STATE)
    b_tiled = jnp.repeat(b_g, tiles_per_group, axis=1).reshape(DEC_BATCH, 1, D_INNER)
    c_exp = jnp.transpose(jnp.repeat(c_g, tiles_per_group, axis=1), (0, 2, 1))
    dec_b = jnp.broadcast_to(dec_s[:, :SSD_HEADS, None], (DEC_BATCH, SSD_HEADS, SSD_STATE))

    g_p, sfin, chist_p, s_new, y_t = _ssd_mix(
        zx_p, cw, cbias, dt_p, cs_p, dt_p.T, cs_p.T, dskip, ng,
        state[j].reshape(DEC_BATCH, D_INNER, SSD_STATE), x_t, dt_t, b_tiled, c_exp, dec_b)
    ss_p = sfin.reshape(BATCH, SSD_HEADS, SSD_HEAD_DIM, SSD_STATE)
    y_s = jnp.transpose(y_t, (0, 2, 1)).reshape(DEC_BATCH, D_INNER)
    g_s = _gate_norm_sample(y_s, xbc_c_s, zx_s, dskip, ng)
    ss_s = s_new.reshape(DEC_BATCH, SSD_HEADS, SSD_HEAD_DIM, SSD_STATE)

    hp, hs = _matmul2(g_p, g_s, [w_out], j, tn=512, tm=1024, res=(hp, hs))
    return hp, hs, chist_p, chist_s, ss_p, ss_s


def kernel(x_prompt, x_sample, mem_prompt, cache_sc, state_ssd_conv, state_ssd, cache_mem_k, cache_mem_v, norm_mix, norm_mem_q, norm_mem_kv, norm_ffn, norm_final, sc_w_in, sc_w_conv, sc_w_out, ssd_w_in, ssd_conv_w, ssd_conv_b, ssd_dt_bias, ssd_a_log, ssd_d, ssd_norm, ssd_w_out, xa_w_q, xa_w_k, xa_w_v, xa_w_o, ffn_w_gate, ffn_w_up, ffn_w_down):
    depth = norm_mix.shape[0]
    hp = x_prompt.reshape(T_P, D_MODEL)
    hs = x_sample.reshape(T_S, D_MODEL)

    mem = mem_prompt.reshape(BATCH * MEM_LEN, D_MODEL)
    k_rows, mem_k_p = _mem_proj(mem, norm_mem_kv, xa_w_k)
    v_rows, mem_v_p = _mem_proj(mem, norm_mem_kv, xa_w_v)

    sc_p, sc_s, sconv_p, sconv_s, ss_p, ss_s = [], [], [], [], [], []
    for i in range(depth):
        if i % 2 == 0:
            hp, hs, hq, a_p, a_s = _short_conv_block(hp, hs, i // 2, norm_mix[i], norm_mem_q[i], cache_sc,
                                                     sc_w_in, sc_w_conv, sc_w_out)
            sc_p.append(a_p)
            sc_s.append(a_s)
        else:
            hp, hs, c_p, c_s, s_p, s_s = _ssd_block(hp, hs, i // 2, norm_mix[i], state_ssd_conv, state_ssd, ssd_w_in,
                                                    ssd_conv_w, ssd_conv_b, ssd_dt_bias, ssd_a_log, ssd_d, ssd_norm,
                                                    ssd_w_out)
            sconv_p.append(c_p)
            sconv_s.append(c_s)
            ss_p.append(s_p)
            ss_s.append(s_s)
            hq = _rmsnorm2(hp, hs, norm_mem_q[i], BF16)
        hp, hs = _mem_attn_ffn_block(hp, hs, hq, i, norm_ffn, xa_w_q, xa_w_o, k_rows, v_rows,
                                     cache_mem_k, cache_mem_v, ffn_w_gate, ffn_w_up, ffn_w_down)

    y_p, y_s = _rmsnorm2(hp, hs, norm_final, F32)

    return (y_p.reshape(BATCH, SEQ, D_MODEL), y_s.reshape(DEC_BATCH, 1, D_MODEL),
            jnp.stack(sc_p), jnp.stack(sc_s), jnp.stack(sconv_p), jnp.stack(sconv_s),
            jnp.stack(ss_p), jnp.stack(ss_s), mem_k_p, mem_v_p)
```

```python
import functools

import jax
import jax.numpy as jnp
from jax import lax
from jax.experimental import pallas as pl
from jax.experimental.pallas import tpu as pltpu

F32 = jnp.float32
BF16 = jnp.bfloat16

D_MODEL = 2048
BATCH = 4
SEQ = 2048
DEC_BATCH = 128
T_P = BATCH * SEQ
T_S = DEC_BATCH
D_INNER = 4096
SSD_HEAD_DIM = 64
SSD_HEADS = 64
SSD_GROUPS = 8
SSD_HPG = 8
SSD_STATE = 128
GROUP_W = SSD_HPG * SSD_HEAD_DIM
BC_W = SSD_GROUPS * SSD_STATE
SSD_CONV_DIM = D_INNER + 2 * BC_W
MEM_LEN = 256
MEM_HEADS = 4
MEM_HEAD_DIM = 512
RMS_EPS = 1e-5
LANES = 128

V7X_VMEM_LIMIT_BYTES = 56 * 1024 * 1024
SSD_CHUNK = 128
SSD_STEP_SEQS = 2
N_TILES = D_INNER // LANES


def _cparams(*sem):
    return pltpu.CompilerParams(dimension_semantics=sem, vmem_limit_bytes=V7X_VMEM_LIMIT_BYTES)


def _silu(x):
    hx = 0.5 * x
    return hx * (1.0 + jnp.tanh(hx))


def _rms(x, g):
    ms = jnp.mean(x * x, axis=-1, keepdims=True)
    return x * lax.rsqrt(ms + RMS_EPS) * g


def _rmsnorm_kernel(xp_ref, xs_ref, g_ref, op_ref, os_ref, *, n_p):
    i = pl.program_id(0)

    @pl.when(i < n_p)
    def _():
        op_ref[...] = _rms(xp_ref[...], g_ref[...]).astype(op_ref.dtype)

    @pl.when(i == n_p)
    def _():
        os_ref[...] = _rms(xs_ref[...], g_ref[...]).astype(os_ref.dtype)


def _rmsnorm2(xp, xs, g, out_dtype, *, tm=1024):
    d = xp.shape[1]
    n_p = xp.shape[0] // tm
    pspec = pl.BlockSpec((tm, d), lambda i: (jnp.minimum(i, n_p - 1), 0))
    sspec = pl.BlockSpec((T_S, d), lambda i: (0, 0))
    return pl.pallas_call(
        functools.partial(_rmsnorm_kernel, n_p=n_p),
        grid=(n_p + 1,),
        in_specs=[pspec, sspec, pl.BlockSpec((1, d), lambda i: (0, 0))],
        out_specs=[pspec, sspec],
        out_shape=[jax.ShapeDtypeStruct(xp.shape, out_dtype), jax.ShapeDtypeStruct(xs.shape, out_dtype)],
        compiler_params=_cparams("arbitrary"),
        name="rmsnorm",
    )(xp, xs, g.reshape(1, d))


def _mem_proj_kernel(x_ref, g_ref, w_ref, o2_ref, o5_ref, wb_ref, *, seqs):
    @pl.when(pl.program_id(1) == 0)
    def _():
        wb_ref[...] = w_ref[...].astype(BF16)

    m = _rms(x_ref[...], g_ref[...]).astype(BF16)
    acc = jnp.dot(m, wb_ref[...], preferred_element_type=F32)
    o2_ref[...] = acc
    for s in range(seqs):
        for h in range(MEM_HEADS):
            o5_ref[s, :, h, :] = acc[MEM_LEN * s:MEM_LEN * (s + 1), MEM_HEAD_DIM * h:MEM_HEAD_DIM * (h + 1)]


def _mem_proj(mem, norm_g, w3, *, seqs=2):
    depth = w3.shape[0]
    rows, d = mem.shape
    tm = seqs * MEM_LEN
    return pl.pallas_call(
        functools.partial(_mem_proj_kernel, seqs=seqs),
        grid=(depth, rows // tm),
        in_specs=[pl.BlockSpec((tm, d), lambda l, i: (i, 0)),
                  pl.BlockSpec((None, 1, d), lambda l, i: (l, 0, 0)),
                  pl.BlockSpec((None, d, d), lambda l, i: (l, 0, 0), pipeline_mode=pl.Buffered(1))],
        out_specs=[pl.BlockSpec((None, tm, d), lambda l, i: (l, i, 0)),
                   pl.BlockSpec((None, seqs, MEM_LEN, MEM_HEADS, MEM_HEAD_DIM), lambda l, i: (l, i, 0, 0, 0))],
        out_shape=[jax.ShapeDtypeStruct((depth, rows, d), F32),
                   jax.ShapeDtypeStruct((depth, rows // MEM_LEN, MEM_LEN, MEM_HEADS, MEM_HEAD_DIM), F32)],
        scratch_shapes=[pltpu.VMEM((d, d), BF16)],
        compiler_params=_cparams("arbitrary", "arbitrary"),
        name="mem_proj",
    )(mem, norm_g.reshape(depth, 1, d), w3)


_NT_DIMS = (((1,), (1,)), ((), ()))


def _mm_kernel(*refs, has_s, has_res, n_w, epilogue, w_transposed):
    it = iter(refs)
    xp_ref = next(it)
    xs_ref = next(it) if has_s else None
    w_refs = [next(it) for _ in range(n_w)]
    rp_ref = next(it) if has_res else None
    rs_ref = next(it) if (has_res and has_s) else None
    op_ref = next(it)
    os_ref = next(it) if has_s else None
    wb_refs = [next(it) for _ in range(n_w)]

    def run(x_ref, r_ref, o_ref):
        x = x_ref[...]
        if w_transposed:
            accs = [lax.dot_general(x, wb_ref[...], _NT_DIMS, preferred_element_type=F32) for wb_ref in wb_refs]
        else:
            accs = [jnp.dot(x, wb_ref[...], preferred_element_type=F32) for wb_ref in wb_refs]
        out = epilogue(*accs)
        if r_ref is not None:
            out = out + r_ref[...]
        o_ref[...] = out.astype(o_ref.dtype)

    @pl.when(pl.program_id(1) == 0)
    def _():
        for w_ref, wb_ref in zip(w_refs, wb_refs):
            wb_ref[...] = w_ref[...].astype(BF16)
        if has_s:
            run(xs_ref, rs_ref, os_ref)

    run(xp_ref, rp_ref, op_ref)


def _identity(a):
    return a


def _swiglu_epilogue(a, u):
    return _silu(a) * u


def _matmul2(xp, xs, ws, layer, *, tn, tm, n_out=None, res=None, out_dtype=F32, epilogue=_identity,
             w_transposed=False, w_buffers=2):
    m, k = xp.shape
    n = n_out if n_out is not None else ws[0].shape[1 if w_transposed else 2]
    has_s = xs is not None
    xp_spec = pl.BlockSpec((tm, k), lambda j, i: (i, 0))
    xs_spec = pl.BlockSpec((T_S, k), lambda j, i: (0, 0))
    w_mode = {} if w_buffers == 2 else {"pipeline_mode": pl.Buffered(w_buffers)}
    if w_transposed:
        w_spec = pl.BlockSpec((None, tn, k), lambda j, i: (layer, j, 0), **w_mode)
    else:
        w_spec = pl.BlockSpec((None, k, tn), lambda j, i: (layer, 0, j), **w_mode)
    op_spec = pl.BlockSpec((tm, tn), lambda j, i: (i, j))
    os_spec = pl.BlockSpec((T_S, tn), lambda j, i: (0, j))

    in_specs, args = [xp_spec], [xp]
    if has_s:
        in_specs.append(xs_spec)
        args.append(xs)
    in_specs += [w_spec] * len(ws)
    args += list(ws)
    if res is not None:
        in_specs.append(op_spec)
        args.append(res[0])
        if has_s:
            in_specs.append(os_spec)
            args.append(res[1])
    out_specs = [op_spec] + ([os_spec] if has_s else [])
    out_shape = [jax.ShapeDtypeStruct((m, n), out_dtype)] + ([jax.ShapeDtypeStruct((T_S, n), out_dtype)] if has_s else [])
    outs = pl.pallas_call(
        functools.partial(_mm_kernel, has_s=has_s, has_res=res is not None, n_w=len(ws), epilogue=epilogue,
                          w_transposed=w_transposed),
        grid=(n // tn, m // tm),
        in_specs=in_specs,
        out_specs=out_specs,
        out_shape=out_shape,
        scratch_shapes=[pltpu.VMEM((tn, k) if w_transposed else (k, tn), BF16) for _ in ws],
        compiler_params=_cparams("arbitrary", "arbitrary"),
        name="matmul",
    )(*args)
    return tuple(outs) if has_s else outs[0]


def _mm_res_norm_kernel(xp_ref, xs_ref, w_ref, rp_ref, rs_ref, g_ref, hp_ref, hs_ref, np_ref, ns_ref, wb_ref):
    def run(x_ref, r_ref, h_ref, n_ref):
        h = jnp.dot(x_ref[...], wb_ref[...], preferred_element_type=F32) + r_ref[...]
        h_ref[...] = h
        n_ref[...] = _rms(h, g_ref[...]).astype(n_ref.dtype)

    @pl.when(pl.program_id(0) == 0)
    def _():
        wb_ref[...] = w_ref[...].astype(BF16)
        run(xs_ref, rs_ref, hs_ref, ns_ref)

    run(xp_ref, rp_ref, hp_ref, np_ref)


def _matmul_res_norm(xp, xs, w3, layer, res, g, *, tm=512):
    m, k = xp.shape
    n = w3.shape[2]
    pspec = pl.BlockSpec((tm, n), lambda i: (i, 0))
    sspec = pl.BlockSpec((T_S, n), lambda i: (0, 0))
    wspec = pl.BlockSpec((None, k, n), lambda i: (layer, 0, 0), pipeline_mode=pl.Buffered(1))
    return pl.pallas_call(
        _mm_res_norm_kernel,
        grid=(m // tm,),
        in_specs=[pl.BlockSpec((tm, k), lambda i: (i, 0)), pl.BlockSpec((T_S, k), lambda i: (0, 0)),
                  wspec, pspec, sspec, pl.BlockSpec((1, n), lambda i: (0, 0))],
        out_specs=[pspec, sspec, pspec, sspec],
        out_shape=[jax.ShapeDtypeStruct((m, n), F32), jax.ShapeDtypeStruct((T_S, n), F32),
                   jax.ShapeDtypeStruct((m, n), BF16), jax.ShapeDtypeStruct((T_S, n), BF16)],
        scratch_shapes=[pltpu.VMEM((k, n), BF16)],
        compiler_params=_cparams("arbitrary"),
        name="matmul_res_norm",
    )(xp, xs, w3, res[0], res[1], g.reshape(1, n))


def _shifted_rows(u, prev8, k):
    row8 = lax.broadcasted_iota(jnp.int32, prev8.shape, 0)
    r = pltpu.roll(u, k, axis=0)
    top = jnp.where(row8 < k, pltpu.roll(prev8, k, axis=0), r[0:8, :])
    return jnp.concatenate([top, r[8:, :]], axis=0)


def _sc_in_kernel(xp_ref, xs_ref, wb_ref, wc_ref, wv_ref, cw_ref, h0_ref, h1_ref,
                  yp_ref, ys_ref, us_ref, hist_ref, wbb_ref, wcb_ref, wvb_ref, prev_ref, *, tiles_per_seq):
    i = pl.program_id(1)
    cw = cw_ref[...]

    def gates(x):
        return (jnp.dot(x, wbb_ref[...], preferred_element_type=F32),
                jnp.dot(x, wcb_ref[...], preferred_element_type=F32),
                jnp.dot(x, wvb_ref[...], preferred_element_type=F32))

    @pl.when(i == 0)
    def _():
        wbb_ref[...] = wb_ref[...].astype(BF16)
        wcb_ref[...] = wc_ref[...].astype(BF16)
        wvb_ref[...] = wv_ref[...].astype(BF16)
        bg, cg, v = gates(xs_ref[...])
        u = cg * v
        conv = cw[2:3, :] * u + cw[1:2, :] * h1_ref[...] + cw[0:1, :] * h0_ref[...]
        ys_ref[...] = (bg * conv).astype(ys_ref.dtype)
        us_ref[...] = u

    @pl.when(i % tiles_per_seq == 0)
    def _():
        prev_ref[...] = jnp.zeros_like(prev_ref)

    bg, cg, v = gates(xp_ref[...])
    u = cg * v
    prev8 = prev_ref[...]
    conv = cw[2:3, :] * u + cw[1:2, :] * _shifted_rows(u, prev8, 1) + cw[0:1, :] * _shifted_rows(u, prev8, 2)
    yp_ref[...] = (bg * conv).astype(yp_ref.dtype)
    tm = u.shape[0]
    prev_ref[...] = u[tm - 8:, :]

    @pl.when(i % tiles_per_seq == tiles_per_seq - 1)
    def _():
        hist_ref[...] = u[tm - 2:, :]


def _sc_in_conv(xp, xs, w_in3, layer, w_conv, h0, h1, *, tn=512, tm=1024):
    k = xp.shape[1]
    nb = D_MODEL // tn
    tiles_per_seq = SEQ // tm

    def wspec(part):
        return pl.BlockSpec((None, k, tn), lambda j, i: (layer, 0, j + part * nb))

    sspec = pl.BlockSpec((T_S, tn), lambda j, i: (0, j))
    pspec = pl.BlockSpec((tm, tn), lambda j, i: (i, j))
    return pl.pallas_call(
        functools.partial(_sc_in_kernel, tiles_per_seq=tiles_per_seq),
        grid=(nb, T_P // tm),
        in_specs=[pl.BlockSpec((tm, k), lambda j, i: (i, 0)),
                  pl.BlockSpec((T_S, k), lambda j, i: (0, 0)),
                  wspec(0), wspec(1), wspec(2),
                  pl.BlockSpec((3, tn), lambda j, i: (0, j)), sspec, sspec],
        out_specs=[pspec, sspec, sspec,
                   pl.BlockSpec((None, 2, tn), lambda j, i: (i // tiles_per_seq, 0, j))],
        out_shape=[jax.ShapeDtypeStruct((T_P, D_MODEL), BF16),
                   jax.ShapeDtypeStruct((T_S, D_MODEL), BF16),
                   jax.ShapeDtypeStruct((T_S, D_MODEL), F32),
                   jax.ShapeDtypeStruct((BATCH, 2, D_MODEL), F32)],
        scratch_shapes=[pltpu.VMEM((k, tn), BF16), pltpu.VMEM((k, tn), BF16), pltpu.VMEM((k, tn), BF16),
                        pltpu.VMEM((8, tn), F32)],
        compiler_params=_cparams("arbitrary", "arbitrary"),
        name="sc_in_conv",
    )(xp, xs, w_in3, w_in3, w_in3, w_conv, h0, h1)


def _ssdconv_sample_kernel(x_ref, h0_ref, h1_ref, h2_ref, w_ref, b_ref, o_ref, raw_ref):
    x = x_ref[...]
    w = w_ref[...]
    conv = w[3:4, :] * x + w[2:3, :] * h2_ref[...] + w[1:2, :] * h1_ref[...] + w[0:1, :] * h0_ref[...]
    o_ref[...] = _silu(conv + b_ref[...])
    raw_ref[...] = x


def _ssdconv_sample(zx, h0, h1, h2, conv_w, conv_b, *, tc=512):
    nb = SSD_CONV_DIM // tc
    off = D_INNER // tc
    hspec = pl.BlockSpec((T_S, tc), lambda j: (0, j))
    return pl.pallas_call(
        _ssdconv_sample_kernel,
        grid=(nb,),
        in_specs=[pl.BlockSpec((T_S, tc), lambda j: (0, j + off)), hspec, hspec, hspec,
                  pl.BlockSpec((4, tc), lambda j: (0, j)),
                  pl.BlockSpec((1, tc), lambda j: (0, j))],
        out_specs=[hspec, hspec],
        out_shape=[jax.ShapeDtypeStruct((T_S, SSD_CONV_DIM), F32),
                   jax.ShapeDtypeStruct((T_S, SSD_CONV_DIM), F32)],
        compiler_params=_cparams("arbitrary"),
        name="ssdconv_sample",
    )(zx, h0, h1, h2, conv_w, conv_b)


def _dt_kernel(xp_ref, xs_ref, w_ref, bias_ref, alog_ref, dtp_ref, csp_ref, dts_ref, decs_ref, *, n_p):
    i = pl.program_id(0)

    def dt_da(x_ref):
        w = w_ref[:SSD_HEADS, :].astype(BF16)
        raw = lax.dot_general(x_ref[...], w, _NT_DIMS, preferred_element_type=F32) + bias_ref[...]
        dt = jnp.maximum(raw, 0.0) + jnp.log1p(jnp.exp(-jnp.abs(raw)))
        return dt, dt * (-jnp.exp(alog_ref[...]))

    @pl.when(i < n_p)
    def _():
        dt, da = dt_da(xp_ref)
        dtp_ref[...] = dt
        row_in_chunk = lax.broadcasted_iota(jnp.int32, da.shape, 0) % SSD_CHUNK
        cs = da
        k = 1
        while k < SSD_CHUNK:
            cs = cs + jnp.where(row_in_chunk >= k, pltpu.roll(cs, k, axis=0), 0.0)
            k *= 2
        csp_ref[...] = cs

    @pl.when(i == n_p)
    def _():
        dt, da = dt_da(xs_ref)
        dts_ref[...] = dt
        decs_ref[...] = jnp.exp(da)


def _dt_prep(hn_p, hn_s, w_in3t, layer, dt_bias, a_log):
    k = hn_p.shape[1]
    tm = 8 * SSD_CHUNK
    n_p = T_P // tm
    dt_block = (D_INNER + SSD_CONV_DIM) // LANES
    pin = pl.BlockSpec((tm, k), lambda i: (jnp.minimum(i, n_p - 1), 0))
    pout = pl.BlockSpec((tm, SSD_HEADS), lambda i: (jnp.minimum(i, n_p - 1), 0))
    sout = pl.BlockSpec((T_S, SSD_HEADS), lambda i: (0, 0))
    one = pl.BlockSpec((1, SSD_HEADS), lambda i: (0, 0))
    return pl.pallas_call(
        functools.partial(_dt_kernel, n_p=n_p),
        grid=(n_p + 1,),
        in_specs=[pin, pl.BlockSpec((T_S, k), lambda i: (0, 0)),
                  pl.BlockSpec((None, LANES, k), lambda i: (layer, dt_block, 0)), one, one],
        out_specs=[pout, pout, sout, sout],
        out_shape=[jax.ShapeDtypeStruct((T_P, SSD_HEADS), F32), jax.ShapeDtypeStruct((T_P, SSD_HEADS), F32),
                   jax.ShapeDtypeStruct((T_S, SSD_HEADS), F32), jax.ShapeDtypeStruct((T_S, SSD_HEADS), F32)],
        compiler_params=_cparams("arbitrary"),
        name="ssd_dt",
    )(hn_p, hn_s, w_in3t, dt_bias.reshape(1, SSD_HEADS), a_log.reshape(1, SSD_HEADS))


def _expand_heads(v, onehot):
    hi = v.astype(BF16)
    lo = (v - hi.astype(F32)).astype(BF16)
    return (jnp.dot(hi, onehot, preferred_element_type=F32)
            + jnp.dot(lo, onehot, preferred_element_type=F32))


def _gate_norm(y, z, ng):
    return _rms(y * _silu(z), ng)


def _ssd_group_chunk(x, bm, cm, z, dtc, csc, dtr, csr, dsk, ng, st):
    q = SSD_CHUNK
    xb = x.astype(BF16)
    bb = bm.astype(BF16)
    cb = cm.astype(BF16)

    lane = lax.broadcasted_iota(jnp.int32, (SSD_HPG, GROUP_W), 1)
    head = lax.broadcasted_iota(jnp.int32, (SSD_HPG, GROUP_W), 0)
    onehot = jnp.where(lane // SSD_HEAD_DIM == head, 1.0, 0.0).astype(BF16)

    cbm = lax.dot_general(cb, bb, (((1,), (1,)), ((), ())), preferred_element_type=F32)
    row = lax.broadcasted_iota(jnp.int32, (q, q), 0)
    col = lax.broadcasted_iota(jnp.int32, (q, q), 1)
    tri = row >= col
    lane128 = lax.broadcasted_iota(jnp.int32, (q, LANES), 1)
    lo_half = lane128 < SSD_HEAD_DIM

    tiles = []
    for m in range(GROUP_W // LANES):
        xt = xb[:, LANES * m:LANES * (m + 1)]
        acc = None
        for j, xm in ((2 * m, jnp.where(lo_half, xt, 0)), (2 * m + 1, jnp.where(lo_half, 0, xt))):
            seg = csc[:, j:j + 1] - csr[j:j + 1, :]
            dec = jnp.exp(jnp.where(tri, seg, -jnp.inf))
            w = (cbm * dec * dtr[j:j + 1, :]).astype(BF16)
            part = jnp.dot(w, xm, preferred_element_type=F32)
            acc = part if acc is None else acc + part
        tiles.append(acc)
    y_diag = jnp.concatenate(tiles, axis=1)

    e_exp = _expand_heads(jnp.exp(csc), onehot)
    y_off = jnp.dot(cb, st.astype(BF16), preferred_element_type=F32) * e_exp

    cs_last = csc[q - 1:q, :]
    te = _expand_heads(jnp.exp(cs_last - csc) * dtc, onehot)
    xw = (x * te).astype(BF16)
    d_st = jnp.dot(bm.T.astype(BF16), xw, preferred_element_type=F32)
    st_new = st * e_exp[q - 1:q, :] + d_st
    y = y_diag + y_off + dsk * x
    return _gate_norm(y, z, ng), st_new


def _causal_conv4_silu(x, prev8, w, bias):
    acc = w[3:4, :] * x + bias
    for k in (1, 2, 3):
        acc = acc + w[3 - k:4 - k, :] * _shifted_rows(x, prev8, k)
    return _silu(acc)


def _ssd_mix_kernel(x_ref, b_ref, c_ref, z_ref, wx_ref, wb_ref, wc_ref, bx_ref, bb_ref, bc_ref,
                    dtc_ref, csc_ref, dtr_ref, csr_ref, dsk_ref, ng_ref,
                    s_ref, xt_ref, dtt_ref, bt_ref, ce_ref, dec_ref,
                    y_ref, sfin_ref, hist_ref, so_ref, yt_ref,
                    st_ref, px_ref, pb_ref, pc_ref):
    q = SSD_CHUNK
    c = pl.program_id(1)

    @pl.when(c == 0)
    def _():
        st_ref[...] = jnp.zeros_like(st_ref)
        px_ref[...] = jnp.zeros_like(px_ref)
        pb_ref[...] = jnp.zeros_like(pb_ref)
        pc_ref[...] = jnp.zeros_like(pc_ref)

    _ssd_step_body(s_ref, xt_ref, dtt_ref, bt_ref, ce_ref, dec_ref, so_ref, yt_ref)

    for gi in range(SSD_GROUPS):
        xs = slice(GROUP_W * gi, GROUP_W * (gi + 1))
        ns = slice(SSD_STATE * gi, SSD_STATE * (gi + 1))
        hs = slice(SSD_HPG * gi, SSD_HPG * (gi + 1))
        x = _causal_conv4_silu(x_ref[:, xs], px_ref[:, xs], wx_ref[:, xs], bx_ref[:, xs])
        bm = _causal_conv4_silu(b_ref[:, ns], pb_ref[:, ns], wb_ref[:, ns], bb_ref[:, ns])
        cm = _causal_conv4_silu(c_ref[:, ns], pc_ref[:, ns], wc_ref[:, ns], bc_ref[:, ns])
        y, st_new = _ssd_group_chunk(x, bm, cm, z_ref[:, xs],
                                     dtc_ref[:, hs], csc_ref[:, hs], dtr_ref[hs, :], csr_ref[hs, :],
                                     dsk_ref[gi], ng_ref[gi], st_ref[gi])
        y_ref[:, xs] = y.astype(y_ref.dtype)
        st_ref[gi] = st_new

    px_ref[...] = x_ref[q - 8:, :]
    pb_ref[...] = b_ref[q - 8:, :]
    pc_ref[...] = c_ref[q - 8:, :]

    @pl.when(c == pl.num_programs(1) - 1)
    def _():
        for gi in range(SSD_GROUPS):
            sfin_ref[GROUP_W * gi:GROUP_W * (gi + 1), :] = st_ref[gi].T
        hist_ref[:, 0:D_INNER] = x_ref[q - 3:, :]
        hist_ref[:, D_INNER:D_INNER + BC_W] = b_ref[q - 3:, :]
        hist_ref[:, D_INNER + BC_W:] = c_ref[q - 3:, :]


def _ssd_mix(zx, conv_w, conv_b, dtc, csc, dtr, csr, dskip, ng, state, x_t, dt_t, b_tiled, c_exp, dec):
    q = SSD_CHUNK
    nc = SEQ // q
    assert BATCH * nc * SSD_STEP_SEQS == DEC_BATCH
    b_blk = D_INNER // BC_W
    small_c = pl.BlockSpec((q, SSD_HEADS), lambda b, c: (b * nc + c, 0))
    small_r = pl.BlockSpec((SSD_HEADS, q), lambda b, c: (0, b * nc + c))
    pspec = pl.BlockSpec((SSD_GROUPS, 1, GROUP_W), lambda b, c: (0, 0, 0))

    def rows(width, col_block):
        return pl.BlockSpec((q, width), lambda b, c: (b * nc + c, col_block))

    def par(nrows, width, col_block):
        return pl.BlockSpec((nrows, width), lambda b, c: (0, col_block))

    def seqs(*dims):
        return pl.BlockSpec((SSD_STEP_SEQS,) + dims, lambda b, c: (b * nc + c, 0, 0))

    tspec = seqs(LANES, N_TILES)
    sspec = seqs(D_INNER, SSD_STATE)
    return pl.pallas_call(
        _ssd_mix_kernel,
        grid=(BATCH, nc),
        in_specs=[rows(D_INNER, 1), rows(BC_W, 2 * b_blk), rows(BC_W, 2 * b_blk + 1), rows(D_INNER, 0),
                  par(4, D_INNER, 0), par(4, BC_W, b_blk), par(4, BC_W, b_blk + 1),
                  par(1, D_INNER, 0), par(1, BC_W, b_blk), par(1, BC_W, b_blk + 1),
                  small_c, small_c, small_r, small_r, pspec, pspec,
                  sspec, tspec, tspec, seqs(1, D_INNER), seqs(SSD_STATE, N_TILES), seqs(SSD_HEADS, SSD_STATE)],
        out_specs=[rows(D_INNER, 0),
                   pl.BlockSpec((D_INNER, SSD_STATE), lambda b, c: (b, 0)),
                   pl.BlockSpec((None, 3, SSD_CONV_DIM), lambda b, c: (b, 0, 0)),
                   sspec, tspec],
        out_shape=[jax.ShapeDtypeStruct((T_P, D_INNER), BF16),
                   jax.ShapeDtypeStruct((BATCH * D_INNER, SSD_STATE), F32),
                   jax.ShapeDtypeStruct((BATCH, 3, SSD_CONV_DIM), F32),
                   jax.ShapeDtypeStruct((DEC_BATCH, D_INNER, SSD_STATE), F32),
                   jax.ShapeDtypeStruct((DEC_BATCH, LANES, N_TILES), F32)],
        scratch_shapes=[pltpu.VMEM((SSD_GROUPS, SSD_STATE, GROUP_W), F32),
                        pltpu.VMEM((8, D_INNER), F32), pltpu.VMEM((8, BC_W), F32), pltpu.VMEM((8, BC_W), F32)],
        compiler_params=_cparams("arbitrary", "arbitrary"),
        name="ssd_mix",
    )(zx, zx, zx, zx, conv_w, conv_w, conv_w, conv_b, conv_b, conv_b, dtc, csc, dtr, csr, dskip, ng,
      state, x_t, dt_t, b_tiled, c_exp, dec)


def _ssd_step_body(s_ref, xt_ref, dtt_ref, bt_ref, ce_ref, dec_ref, so_ref, yt_ref):
    tile_of_lane = lax.broadcasted_iota(jnp.int32, (N_TILES, D_INNER), 1) // LANES
    row32 = lax.broadcasted_iota(jnp.int32, (N_TILES, D_INNER), 0)
    lane32 = lax.broadcasted_iota(jnp.int32, (SSD_STATE, N_TILES), 1)
    for b in range(SSD_STEP_SEQS):
        xdt = (xt_ref[b] * dtt_ref[b]).astype(BF16)
        b_big = jnp.where(tile_of_lane == row32, bt_ref[b], 0.0).astype(BF16)
        upd = jnp.dot(xdt, b_big, preferred_element_type=F32)

        ce = ce_ref[b]
        s_tiles, c_tiles = [], []
        for r in range(N_TILES):
            dtile = jnp.concatenate(
                [jnp.broadcast_to(dec_ref[b, 2 * r:2 * r + 1, :], (SSD_HEAD_DIM, SSD_STATE)),
                 jnp.broadcast_to(dec_ref[b, 2 * r + 1:2 * r + 2, :], (SSD_HEAD_DIM, SSD_STATE))], axis=0)
            s_n = s_ref[b, LANES * r:LANES * (r + 1), :] * dtile + upd[:, LANES * r:LANES * (r + 1)]
            so_ref[b, LANES * r:LANES * (r + 1), :] = s_n
            s_tiles.append(s_n.astype(BF16))
            c_tiles.append(jnp.where(lane32 == r, ce, 0.0).astype(BF16))
        s_big = jnp.concatenate(s_tiles, axis=1)
        c_big = jnp.concatenate(c_tiles, axis=0)
        yt_ref[b] = jnp.dot(s_big, c_big, preferred_element_type=F32)


def _gate_norm_kernel(y_ref, x_ref, z_ref, dsk_ref, ng_ref, o_ref):
    y = y_ref[...] + dsk_ref[...] * x_ref[...]
    o_ref[...] = _gate_norm(y, z_ref[...], ng_ref[...]).astype(o_ref.dtype)


def _gate_norm_sample(y_s, xbc_c_s, zx_s, dskip, ng):
    gspec = pl.BlockSpec((T_S, GROUP_W), lambda g: (0, g))
    pspec = pl.BlockSpec((None, 1, GROUP_W), lambda g: (g, 0, 0))
    return pl.pallas_call(
        _gate_norm_kernel,
        grid=(SSD_GROUPS,),
        in_specs=[gspec, gspec, gspec, pspec, pspec],
        out_specs=gspec,
        out_shape=jax.ShapeDtypeStruct((T_S, D_INNER), BF16),
        compiler_params=_cparams("arbitrary"),
        name="ssd_gate_norm",
    )(y_s, xbc_c_s, zx_s, dskip, ng)


def _xattn_kernel(qp_ref, kp_ref, vp_ref, qs_ref, kc_ref, vc_ref, op_ref, os_ref, *, bb):
    scale = MEM_HEAD_DIM ** -0.5
    for b in range(bb):
        for h in range(MEM_HEADS):
            kh = kc_ref[b, :, h, :]
            s = jnp.sum(kh * qs_ref[b, h:h + 1, :], axis=-1, keepdims=True) * scale
            m = jnp.max(s, axis=0, keepdims=True)
            e = jnp.exp(s - m)
            p = e / jnp.sum(e, axis=0, keepdims=True)
            os_ref[b, h:h + 1, :] = jnp.sum(p * vc_ref[b, :, h, :], axis=0, keepdims=True)

    for h in range(MEM_HEADS):
        sl = slice(MEM_HEAD_DIM * h, MEM_HEAD_DIM * (h + 1))
        kh = kp_ref[:, sl].astype(BF16)
        vh = vp_ref[:, sl].astype(BF16)
        s = lax.dot_general(qp_ref[:, sl], kh, _NT_DIMS, preferred_element_type=F32) * scale
        m = jnp.max(s, axis=-1, keepdims=True)
        e = jnp.exp(s - m)
        p = (e / jnp.sum(e, axis=-1, keepdims=True)).astype(BF16)
        op_ref[:, sl] = jnp.dot(p, vh, preferred_element_type=F32).astype(op_ref.dtype)


def _xattn(q_p, k_rows, v_rows, q_s, kc, vc, layer, *, bb=4):
    steps = DEC_BATCH // bb
    tq = T_P // steps
    tiles_per_seq = SEQ // tq
    kspec = pl.BlockSpec((None, MEM_LEN, D_MODEL), lambda i: (layer, i // tiles_per_seq, 0))
    cspec = pl.BlockSpec((None, bb, MEM_LEN, MEM_HEADS, MEM_HEAD_DIM), lambda i: (layer, i, 0, 0, 0))
    qp_spec = pl.BlockSpec((tq, D_MODEL), lambda i: (i, 0))
    qs_spec = pl.BlockSpec((bb, MEM_HEADS, MEM_HEAD_DIM), lambda i: (i, 0, 0))
    return pl.pallas_call(
        functools.partial(_xattn_kernel, bb=bb),
        grid=(steps,),
        in_specs=[qp_spec, kspec, kspec, qs_spec, cspec, cspec],
        out_specs=[qp_spec, qs_spec],
        out_shape=[jax.ShapeDtypeStruct((T_P, D_MODEL), BF16),
                   jax.ShapeDtypeStruct((DEC_BATCH, MEM_HEADS, MEM_HEAD_DIM), F32)],
        compiler_params=_cparams("arbitrary"),
        name="xattn",
    )(q_p, k_rows, v_rows, q_s, kc, vc)


def _mem_attn_ffn_block(hp, hs, hq, i, norm_ffn, xa_w_q, xa_w_o, k_p, v_p, kc, vc, w_gate, w_up, w_down):
    q_p, q_s = _matmul2(hq[0], hq[1], [xa_w_q], i, tn=1024, tm=1024, out_dtype=BF16)
    o_p, o_s = _xattn(q_p, k_p, v_p, q_s.astype(F32).reshape(DEC_BATCH, MEM_HEADS, MEM_HEAD_DIM), kc, vc, i)
    o_s = o_s.reshape(DEC_BATCH, D_MODEL).astype(BF16)
    hp, hs, hf_p, hf_s = _matmul_res_norm(o_p, o_s, xa_w_o, i, (hp, hs), norm_ffn[i])
    a_p, a_s = _matmul2(hf_p, hf_s, [w_gate, w_up], i, tn=512, tm=1024, out_dtype=BF16, epilogue=_swiglu_epilogue)
    return _matmul2(a_p, a_s, [w_down], i, tn=1024, tm=256, res=(hp, hs), w_buffers=1)


def _short_conv_block(hp, hs, a, norm_g, norm_next, cache_sc, sc_w_in, sc_w_conv, sc_w_out):
    hn_p, hn_s = _rmsnorm2(hp, hs, norm_g, BF16)
    h0, h1 = cache_sc[a, :, 0, :], cache_sc[a, :, 1, :]
    y_p, y_s, u_s, hist_p = _sc_in_conv(hn_p, hn_s, sc_w_in, a, sc_w_conv[a], h0, h1)
    hp, hs, hq_p, hq_s = _matmul_res_norm(y_p, y_s, sc_w_out, a, (hp, hs), norm_next)
    return hp, hs, (hq_p, hq_s), hist_p, jnp.stack([h1, u_s], axis=1)


def _ssd_block(hp, hs, j, norm_g, conv_hist, state, ssd_w_in, conv_w, conv_b, dt_bias, a_log, d_skip, norm_ssd, w_out):
    hn_p, hn_s = _rmsnorm2(hp, hs, norm_g, BF16)
    w_in_t = jnp.swapaxes(ssd_w_in, 1, 2)
    zx_p, zx_s = _matmul2(hn_p, hn_s, [w_in_t], j, tn=1024, tm=1024, n_out=D_INNER + SSD_CONV_DIM,
                          w_transposed=True)

    dt_p, cs_p, dt_s, dec_s = _dt_prep(hn_p, hn_s, w_in_t, j, dt_bias[j], a_log[j])

    cw, cbias = conv_w[j], conv_b[j].reshape(1, SSD_CONV_DIM)
    ch = conv_hist[j]
    xbc_c_s, raw_s = _ssdconv_sample(zx_s, ch[:, 0, :], ch[:, 1, :], ch[:, 2, :], cw, cbias)
    chist_s = jnp.stack([ch[:, 1, :], ch[:, 2, :], raw_s], axis=1)

    dskip = jnp.repeat(d_skip[j], SSD_HEAD_DIM).reshape(SSD_GROUPS, 1, GROUP_W)
    ng = norm_ssd[j].reshape(SSD_GROUPS, 1, GROUP_W)

    def tile_major(v):
        return jnp.transpose(v.reshape(DEC_BATCH, N_TILES, LANES), (0, 2, 1))

    tiles_per_group = GROUP_W // LANES
    x_t = tile_major(xbc_c_s[:, :D_INNER])
    dt_t = tile_major(jnp.repeat(dt_s[:, :SSD_HEADS], SSD_HEAD_DIM, axis=1))
    b_g = xbc_c_s[:, D_INNER:D_INNER + BC_W].reshape(DEC_BATCH, SSD_GROUPS, SSD_STATE)
    c_g = xbc_c_s[:, D_INNER + BC_W:].reshape(DEC_BATCH, SSD_GROUPS, SSD_STATE)
    b_tiled = jnp.repeat(b_g, tiles_per_group, axis=1).reshape(DEC_BATCH, 1, D_INNER)
    c_exp = jnp.transpose(jnp.repeat(c_g, tiles_per_group, axis=1), (0, 2, 1))
    dec_b = jnp.broadcast_to(dec_s[:, :SSD_HEADS, None], (DEC_BATCH, SSD_HEADS, SSD_STATE))

    g_p, sfin, chist_p, s_new, y_t = _ssd_mix(
        zx_p, cw, cbias, dt_p, cs_p, dt_p.T, cs_p.T, dskip, ng,
        state[j].reshape(DEC_BATCH, D_INNER, SSD_STATE), x_t, dt_t, b_tiled, c_exp, dec_b)
    ss_p = sfin.reshape(BATCH, SSD_HEADS, SSD_HEAD_DIM, SSD_STATE)
    y_s = jnp.transpose(y_t, (0, 2, 1)).reshape(DEC_BATCH, D_INNER)
    g_s = _gate_norm_sample(y_s, xbc_c_s, zx_s, dskip, ng)
    ss_s = s_new.reshape(DEC_BATCH, SSD_HEADS, SSD_HEAD_DIM, SSD_STATE)

    hp, hs = _matmul2(g_p, g_s, [w_out], j, tn=1024, tm=512, res=(hp, hs), w_buffers=1)
    return hp, hs, chist_p, chist_s, ss_p, ss_s


def kernel(x_prompt, x_sample, mem_prompt, cache_sc, state_ssd_conv, state_ssd, cache_mem_k, cache_mem_v, norm_mix, norm_mem_q, norm_mem_kv, norm_ffn, norm_final, sc_w_in, sc_w_conv, sc_w_out, ssd_w_in, ssd_conv_w, ssd_conv_b, ssd_dt_bias, ssd_a_log, ssd_d, ssd_norm, ssd_w_out, xa_w_q, xa_w_k, xa_w_v, xa_w_o, ffn_w_gate, ffn_w_up, ffn_w_down):
    depth = norm_mix.shape[0]
    hp = x_prompt.reshape(T_P, D_MODEL)
    hs = x_sample.reshape(T_S, D_MODEL)

    mem = mem_prompt.reshape(BATCH * MEM_LEN, D_MODEL)
    k_rows, mem_k_p = _mem_proj(mem, norm_mem_kv, xa_w_k)
    v_rows, mem_v_p = _mem_proj(mem, norm_mem_kv, xa_w_v)

    sc_p, sc_s, sconv_p, sconv_s, ss_p, ss_s = [], [], [], [], [], []
    for i in range(depth):
        if i % 2 == 0:
            hp, hs, hq, a_p, a_s = _short_conv_block(hp, hs, i // 2, norm_mix[i], norm_mem_q[i], cache_sc,
                                                     sc_w_in, sc_w_conv, sc_w_out)
            sc_p.append(a_p)
            sc_s.append(a_s)
        else:
            hp, hs, c_p, c_s, s_p, s_s = _ssd_block(hp, hs, i // 2, norm_mix[i], state_ssd_conv, state_ssd, ssd_w_in,
                                                    ssd_conv_w, ssd_conv_b, ssd_dt_bias, ssd_a_log, ssd_d, ssd_norm,
                                                    ssd_w_out)
            sconv_p.append(c_p)
            sconv_s.append(c_s)
            ss_p.append(s_p)
            ss_s.append(s_s)
            hq = _rmsnorm2(hp, hs, norm_mem_q[i], BF16)
        hp, hs = _mem_attn_ffn_block(hp, hs, hq, i, norm_ffn, xa_w_q, xa_w_o, k_rows, v_rows,
                                     cache_mem_k, cache_mem_v, ffn_w_gate, ffn_w_up, ffn_w_down)

    y_p, y_s = _rmsnorm2(hp, hs, norm_final, F32)

    return (y_p.reshape(BATCH, SEQ, D_MODEL), y_s.reshape(DEC_BATCH, 1, D_MODEL),
            jnp.stack(sc_p), jnp.stack(sc_s), jnp.stack(sconv_p), jnp.stack(sconv_s),
            jnp.stack(ss_p), jnp.stack(ss_s), mem_k_p, mem_v_p)
```

```python
import functools

import jax
import jax.numpy as jnp
from jax import lax
from jax.experimental import pallas as pl
from jax.experimental.pallas import tpu as pltpu

F32 = jnp.float32
BF16 = jnp.bfloat16

D_MODEL = 2048
BATCH = 4
SEQ = 2048
DEC_BATCH = 128
T_P = BATCH * SEQ
T_S = DEC_BATCH
D_INNER = 4096
SSD_HEAD_DIM = 64
SSD_HEADS = 64
SSD_GROUPS = 8
SSD_HPG = 8
SSD_STATE = 128
GROUP_W = SSD_HPG * SSD_HEAD_DIM
BC_W = SSD_GROUPS * SSD_STATE
SSD_CONV_DIM = D_INNER + 2 * BC_W
MEM_LEN = 256
MEM_HEADS = 4
MEM_HEAD_DIM = 512
RMS_EPS = 1e-5
LANES = 128

V7X_VMEM_LIMIT_BYTES = 56 * 1024 * 1024
SSD_CHUNK = 128
SSD_STEP_SEQS = 2
N_TILES = D_INNER // LANES


def _cparams(*sem):
    return pltpu.CompilerParams(dimension_semantics=sem, vmem_limit_bytes=V7X_VMEM_LIMIT_BYTES)


def _silu(x):
    hx = 0.5 * x
    return hx * (1.0 + jnp.tanh(hx))


def _rms(x, g):
    ms = jnp.mean(x * x, axis=-1, keepdims=True)
    return x * lax.rsqrt(ms + RMS_EPS) * g


def _rmsnorm_kernel(xp_ref, xs_ref, g_ref, op_ref, os_ref, *, n_p):
    i = pl.program_id(0)

    @pl.when(i < n_p)
    def _():
        op_ref[...] = _rms(xp_ref[...], g_ref[...]).astype(op_ref.dtype)

    @pl.when(i == n_p)
    def _():
        os_ref[...] = _rms(xs_ref[...], g_ref[...]).astype(os_ref.dtype)


def _rmsnorm2(xp, xs, g, out_dtype, *, tm=1024):
    d = xp.shape[1]
    n_p = xp.shape[0] // tm
    pspec = pl.BlockSpec((tm, d), lambda i: (jnp.minimum(i, n_p - 1), 0))
    sspec = pl.BlockSpec((T_S, d), lambda i: (0, 0))
    return pl.pallas_call(
        functools.partial(_rmsnorm_kernel, n_p=n_p),
        grid=(n_p + 1,),
        in_specs=[pspec, sspec, pl.BlockSpec((1, d), lambda i: (0, 0))],
        out_specs=[pspec, sspec],
        out_shape=[jax.ShapeDtypeStruct(xp.shape, out_dtype), jax.ShapeDtypeStruct(xs.shape, out_dtype)],
        compiler_params=_cparams("arbitrary"),
        name="rmsnorm",
    )(xp, xs, g.reshape(1, d))


def _mem_proj_kernel(x_ref, g_ref, w_ref, o2_ref, o5_ref, wb_ref, *, seqs):
    @pl.when(pl.program_id(1) == 0)
    def _():
        wb_ref[...] = w_ref[...].astype(BF16)

    m = _rms(x_ref[...], g_ref[...]).astype(BF16)
    acc = jnp.dot(m, wb_ref[...], preferred_element_type=F32)
    o2_ref[...] = acc.astype(o2_ref.dtype)
    for s in range(seqs):
        for h in range(MEM_HEADS):
            o5_ref[s, :, h, :] = acc[MEM_LEN * s:MEM_LEN * (s + 1), MEM_HEAD_DIM * h:MEM_HEAD_DIM * (h + 1)]


def _mem_proj(mem, norm_g, w3, *, seqs=2):
    depth = w3.shape[0]
    rows, d = mem.shape
    tm = seqs * MEM_LEN
    return pl.pallas_call(
        functools.partial(_mem_proj_kernel, seqs=seqs),
        grid=(depth, rows // tm),
        in_specs=[pl.BlockSpec((tm, d), lambda l, i: (i, 0)),
                  pl.BlockSpec((None, 1, d), lambda l, i: (l, 0, 0)),
                  pl.BlockSpec((None, d, d), lambda l, i: (l, 0, 0), pipeline_mode=pl.Buffered(1))],
        out_specs=[pl.BlockSpec((None, tm, d), lambda l, i: (l, i, 0)),
                   pl.BlockSpec((None, seqs, MEM_LEN, MEM_HEADS, MEM_HEAD_DIM), lambda l, i: (l, i, 0, 0, 0))],
        out_shape=[jax.ShapeDtypeStruct((depth, rows, d), BF16),
                   jax.ShapeDtypeStruct((depth, rows // MEM_LEN, MEM_LEN, MEM_HEADS, MEM_HEAD_DIM), F32)],
        scratch_shapes=[pltpu.VMEM((d, d), BF16)],
        compiler_params=_cparams("arbitrary", "arbitrary"),
        name="mem_proj",
    )(mem, norm_g.reshape(depth, 1, d), w3)


_NT_DIMS = (((1,), (1,)), ((), ()))


def _mm_kernel(*refs, has_s, has_res, n_w, epilogue, w_transposed):
    it = iter(refs)
    xp_ref = next(it)
    xs_ref = next(it) if has_s else None
    w_refs = [next(it) for _ in range(n_w)]
    rp_ref = next(it) if has_res else None
    rs_ref = next(it) if (has_res and has_s) else None
    op_ref = next(it)
    os_ref = next(it) if has_s else None
    wb_refs = [next(it) for _ in range(n_w)]

    def run(x_ref, r_ref, o_ref):
        x = x_ref[...]
        if w_transposed:
            accs = [lax.dot_general(x, wb_ref[...], _NT_DIMS, preferred_element_type=F32) for wb_ref in wb_refs]
        else:
            accs = [jnp.dot(x, wb_ref[...], preferred_element_type=F32) for wb_ref in wb_refs]
        out = epilogue(*accs)
        if r_ref is not None:
            out = out + r_ref[...]
        o_ref[...] = out.astype(o_ref.dtype)

    @pl.when(pl.program_id(1) == 0)
    def _():
        for w_ref, wb_ref in zip(w_refs, wb_refs):
            wb_ref[...] = w_ref[...].astype(BF16)
        if has_s:
            run(xs_ref, rs_ref, os_ref)

    run(xp_ref, rp_ref, op_ref)


def _identity(a):
    return a


def _swiglu_epilogue(a, u):
    return _silu(a) * u


def _matmul2(xp, xs, ws, layer, *, tn, tm, n_out=None, res=None, out_dtype=F32, epilogue=_identity,
             w_transposed=False, w_buffers=2):
    m, k = xp.shape
    n = n_out if n_out is not None else ws[0].shape[1 if w_transposed else 2]
    has_s = xs is not None
    xp_spec = pl.BlockSpec((tm, k), lambda j, i: (i, 0))
    xs_spec = pl.BlockSpec((T_S, k), lambda j, i: (0, 0))
    w_mode = {} if w_buffers == 2 else {"pipeline_mode": pl.Buffered(w_buffers)}
    if w_transposed:
        w_spec = pl.BlockSpec((None, tn, k), lambda j, i: (layer, j, 0), **w_mode)
    else:
        w_spec = pl.BlockSpec((None, k, tn), lambda j, i: (layer, 0, j), **w_mode)
    op_spec = pl.BlockSpec((tm, tn), lambda j, i: (i, j))
    os_spec = pl.BlockSpec((T_S, tn), lambda j, i: (0, j))

    in_specs, args = [xp_spec], [xp]
    if has_s:
        in_specs.append(xs_spec)
        args.append(xs)
    in_specs += [w_spec] * len(ws)
    args += list(ws)
    if res is not None:
        in_specs.append(op_spec)
        args.append(res[0])
        if has_s:
            in_specs.append(os_spec)
            args.append(res[1])
    out_specs = [op_spec] + ([os_spec] if has_s else [])
    out_shape = [jax.ShapeDtypeStruct((m, n), out_dtype)] + ([jax.ShapeDtypeStruct((T_S, n), out_dtype)] if has_s else [])
    outs = pl.pallas_call(
        functools.partial(_mm_kernel, has_s=has_s, has_res=res is not None, n_w=len(ws), epilogue=epilogue,
                          w_transposed=w_transposed),
        grid=(n // tn, m // tm),
        in_specs=in_specs,
        out_specs=out_specs,
        out_shape=out_shape,
        scratch_shapes=[pltpu.VMEM((tn, k) if w_transposed else (k, tn), BF16) for _ in ws],
        compiler_params=_cparams("arbitrary", "arbitrary"),
        name="matmul",
    )(*args)
    return tuple(outs) if has_s else outs[0]


def _mm_res_norm_kernel(xp_ref, xs_ref, w_ref, rp_ref, rs_ref, g_ref, hp_ref, hs_ref, np_ref, ns_ref, wb_ref):
    def run(x_ref, r_ref, h_ref, n_ref):
        h = jnp.dot(x_ref[...], wb_ref[...], preferred_element_type=F32) + r_ref[...]
        h_ref[...] = h
        n_ref[...] = _rms(h, g_ref[...]).astype(n_ref.dtype)

    @pl.when(pl.program_id(0) == 0)
    def _():
        wb_ref[...] = w_ref[...].astype(BF16)
        run(xs_ref, rs_ref, hs_ref, ns_ref)

    run(xp_ref, rp_ref, hp_ref, np_ref)


def _matmul_res_norm(xp, xs, w3, layer, res, g, *, tm=512):
    m, k = xp.shape
    n = w3.shape[2]
    pspec = pl.BlockSpec((tm, n), lambda i: (i, 0))
    sspec = pl.BlockSpec((T_S, n), lambda i: (0, 0))
    wspec = pl.BlockSpec((None, k, n), lambda i: (layer, 0, 0), pipeline_mode=pl.Buffered(1))
    return pl.pallas_call(
        _mm_res_norm_kernel,
        grid=(m // tm,),
        in_specs=[pl.BlockSpec((tm, k), lambda i: (i, 0)), pl.BlockSpec((T_S, k), lambda i: (0, 0)),
                  wspec, pspec, sspec, pl.BlockSpec((1, n), lambda i: (0, 0))],
        out_specs=[pspec, sspec, pspec, sspec],
        out_shape=[jax.ShapeDtypeStruct((m, n), F32), jax.ShapeDtypeStruct((T_S, n), F32),
                   jax.ShapeDtypeStruct((m, n), BF16), jax.ShapeDtypeStruct((T_S, n), BF16)],
        scratch_shapes=[pltpu.VMEM((k, n), BF16)],
        compiler_params=_cparams("arbitrary"),
        name="matmul_res_norm",
    )(xp, xs, w3, res[0], res[1], g.reshape(1, n))


def _shifted_rows(u, prev8, k):
    row8 = lax.broadcasted_iota(jnp.int32, prev8.shape, 0)
    r = pltpu.roll(u, k, axis=0)
    top = jnp.where(row8 < k, pltpu.roll(prev8, k, axis=0), r[0:8, :])
    return jnp.concatenate([top, r[8:, :]], axis=0)


def _sc_in_kernel(xp_ref, xs_ref, wb_ref, wc_ref, wv_ref, cw_ref, h0_ref, h1_ref,
                  yp_ref, ys_ref, us_ref, hist_ref, wbb_ref, wcb_ref, wvb_ref, prev_ref, *, tiles_per_seq):
    i = pl.program_id(1)
    cw = cw_ref[...]

    def gates(x):
        return (jnp.dot(x, wbb_ref[...], preferred_element_type=F32),
                jnp.dot(x, wcb_ref[...], preferred_element_type=F32),
                jnp.dot(x, wvb_ref[...], preferred_element_type=F32))

    @pl.when(i == 0)
    def _():
        wbb_ref[...] = wb_ref[...].astype(BF16)
        wcb_ref[...] = wc_ref[...].astype(BF16)
        wvb_ref[...] = wv_ref[...].astype(BF16)
        bg, cg, v = gates(xs_ref[...])
        u = cg * v
        conv = cw[2:3, :] * u + cw[1:2, :] * h1_ref[...] + cw[0:1, :] * h0_ref[...]
        ys_ref[...] = (bg * conv).astype(ys_ref.dtype)
        us_ref[...] = u

    @pl.when(i % tiles_per_seq == 0)
    def _():
        prev_ref[...] = jnp.zeros_like(prev_ref)

    bg, cg, v = gates(xp_ref[...])
    u = cg * v
    prev8 = prev_ref[...]
    conv = cw[2:3, :] * u + cw[1:2, :] * _shifted_rows(u, prev8, 1) + cw[0:1, :] * _shifted_rows(u, prev8, 2)
    yp_ref[...] = (bg * conv).astype(yp_ref.dtype)
    tm = u.shape[0]
    prev_ref[...] = u[tm - 8:, :]

    @pl.when(i % tiles_per_seq == tiles_per_seq - 1)
    def _():
        hist_ref[...] = u[tm - 2:, :]


def _sc_in_conv(xp, xs, w_in3, layer, w_conv, h0, h1, *, tn=512, tm=1024):
    k = xp.shape[1]
    nb = D_MODEL // tn
    tiles_per_seq = SEQ // tm

    def wspec(part):
        return pl.BlockSpec((None, k, tn), lambda j, i: (layer, 0, j + part * nb))

    sspec = pl.BlockSpec((T_S, tn), lambda j, i: (0, j))
    pspec = pl.BlockSpec((tm, tn), lambda j, i: (i, j))
    return pl.pallas_call(
        functools.partial(_sc_in_kernel, tiles_per_seq=tiles_per_seq),
        grid=(nb, T_P // tm),
        in_specs=[pl.BlockSpec((tm, k), lambda j, i: (i, 0)),
                  pl.BlockSpec((T_S, k), lambda j, i: (0, 0)),
                  wspec(0), wspec(1), wspec(2),
                  pl.BlockSpec((3, tn), lambda j, i: (0, j)), sspec, sspec],
        out_specs=[pspec, sspec, sspec,
                   pl.BlockSpec((None, 2, tn), lambda j, i: (i // tiles_per_seq, 0, j))],
        out_shape=[jax.ShapeDtypeStruct((T_P, D_MODEL), BF16),
                   jax.ShapeDtypeStruct((T_S, D_MODEL), BF16),
                   jax.ShapeDtypeStruct((T_S, D_MODEL), F32),
                   jax.ShapeDtypeStruct((BATCH, 2, D_MODEL), F32)],
        scratch_shapes=[pltpu.VMEM((k, tn), BF16), pltpu.VMEM((k, tn), BF16), pltpu.VMEM((k, tn), BF16),
                        pltpu.VMEM((8, tn), F32)],
        compiler_params=_cparams("arbitrary", "arbitrary"),
        name="sc_in_conv",
    )(xp, xs, w_in3, w_in3, w_in3, w_conv, h0, h1)


def _ssdconv_sample_kernel(x_ref, h0_ref, h1_ref, h2_ref, w_ref, b_ref, o_ref, raw_ref):
    x = x_ref[...]
    w = w_ref[...]
    conv = w[3:4, :] * x + w[2:3, :] * h2_ref[...] + w[1:2, :] * h1_ref[...] + w[0:1, :] * h0_ref[...]
    o_ref[...] = _silu(conv + b_ref[...])
    raw_ref[...] = x


def _ssdconv_sample(zx, h0, h1, h2, conv_w, conv_b, *, tc=512):
    nb = SSD_CONV_DIM // tc
    off = D_INNER // tc
    hspec = pl.BlockSpec((T_S, tc), lambda j: (0, j))
    return pl.pallas_call(
        _ssdconv_sample_kernel,
        grid=(nb,),
        in_specs=[pl.BlockSpec((T_S, tc), lambda j: (0, j + off)), hspec, hspec, hspec,
                  pl.BlockSpec((4, tc), lambda j: (0, j)),
                  pl.BlockSpec((1, tc), lambda j: (0, j))],
        out_specs=[hspec, hspec],
        out_shape=[jax.ShapeDtypeStruct((T_S, SSD_CONV_DIM), F32),
                   jax.ShapeDtypeStruct((T_S, SSD_CONV_DIM), F32)],
        compiler_params=_cparams("arbitrary"),
        name="ssdconv_sample",
    )(zx, h0, h1, h2, conv_w, conv_b)


def _dt_kernel(xp_ref, xs_ref, w_ref, bias_ref, alog_ref, dtp_ref, csp_ref, dts_ref, decs_ref, *, n_p):
    i = pl.program_id(0)

    def dt_da(x_ref):
        w = w_ref[:SSD_HEADS, :].astype(BF16)
        raw = lax.dot_general(x_ref[...], w, _NT_DIMS, preferred_element_type=F32) + bias_ref[...]
        dt = jnp.maximum(raw, 0.0) + jnp.log1p(jnp.exp(-jnp.abs(raw)))
        return dt, dt * (-jnp.exp(alog_ref[...]))

    @pl.when(i < n_p)
    def _():
        dt, da = dt_da(xp_ref)
        dtp_ref[...] = dt
        row_in_chunk = lax.broadcasted_iota(jnp.int32, da.shape, 0) % SSD_CHUNK
        cs = da
        k = 1
        while k < SSD_CHUNK:
            cs = cs + jnp.where(row_in_chunk >= k, pltpu.roll(cs, k, axis=0), 0.0)
            k *= 2
        csp_ref[...] = cs

    @pl.when(i == n_p)
    def _():
        dt, da = dt_da(xs_ref)
        dts_ref[...] = dt
        decs_ref[...] = jnp.exp(da)


def _dt_prep(hn_p, hn_s, w_in3t, layer, dt_bias, a_log):
    k = hn_p.shape[1]
    tm = 8 * SSD_CHUNK
    n_p = T_P // tm
    dt_block = (D_INNER + SSD_CONV_DIM) // LANES
    pin = pl.BlockSpec((tm, k), lambda i: (jnp.minimum(i, n_p - 1), 0))
    pout = pl.BlockSpec((tm, SSD_HEADS), lambda i: (jnp.minimum(i, n_p - 1), 0))
    sout = pl.BlockSpec((T_S, SSD_HEADS), lambda i: (0, 0))
    one = pl.BlockSpec((1, SSD_HEADS), lambda i: (0, 0))
    return pl.pallas_call(
        functools.partial(_dt_kernel, n_p=n_p),
        grid=(n_p + 1,),
        in_specs=[pin, pl.BlockSpec((T_S, k), lambda i: (0, 0)),
                  pl.BlockSpec((None, LANES, k), lambda i: (layer, dt_block, 0)), one, one],
        out_specs=[pout, pout, sout, sout],
        out_shape=[jax.ShapeDtypeStruct((T_P, SSD_HEADS), F32), jax.ShapeDtypeStruct((T_P, SSD_HEADS), F32),
                   jax.ShapeDtypeStruct((T_S, SSD_HEADS), F32), jax.ShapeDtypeStruct((T_S, SSD_HEADS), F32)],
        compiler_params=_cparams("arbitrary"),
        name="ssd_dt",
    )(hn_p, hn_s, w_in3t, dt_bias.reshape(1, SSD_HEADS), a_log.reshape(1, SSD_HEADS))


def _expand_heads(v, onehot):
    hi = v.astype(BF16)
    lo = (v - hi.astype(F32)).astype(BF16)
    return (jnp.dot(hi, onehot, preferred_element_type=F32)
            + jnp.dot(lo, onehot, preferred_element_type=F32))


def _gate_norm(y, z, ng):
    return _rms(y * _silu(z), ng)


def _ssd_group_chunk(x, bm, cm, z, dtc, csc, dtr, csr, dsk, ng, st):
    q = SSD_CHUNK
    xb = x.astype(BF16)
    bb = bm.astype(BF16)
    cb = cm.astype(BF16)

    lane = lax.broadcasted_iota(jnp.int32, (SSD_HPG, GROUP_W), 1)
    head = lax.broadcasted_iota(jnp.int32, (SSD_HPG, GROUP_W), 0)
    onehot = jnp.where(lane // SSD_HEAD_DIM == head, 1.0, 0.0).astype(BF16)

    cbm = lax.dot_general(cb, bb, (((1,), (1,)), ((), ())), preferred_element_type=F32)
    row = lax.broadcasted_iota(jnp.int32, (q, q), 0)
    col = lax.broadcasted_iota(jnp.int32, (q, q), 1)
    tri = row >= col
    lane128 = lax.broadcasted_iota(jnp.int32, (q, LANES), 1)
    lo_half = lane128 < SSD_HEAD_DIM

    tiles = []
    for m in range(GROUP_W // LANES):
        xt = xb[:, LANES * m:LANES * (m + 1)]
        acc = None
        for j, xm in ((2 * m, jnp.where(lo_half, xt, 0)), (2 * m + 1, jnp.where(lo_half, 0, xt))):
            seg = csc[:, j:j + 1] - csr[j:j + 1, :]
            dec = jnp.exp(jnp.where(tri, seg, -jnp.inf))
            w = (cbm * dec * dtr[j:j + 1, :]).astype(BF16)
            part = jnp.dot(w, xm, preferred_element_type=F32)
            acc = part if acc is None else acc + part
        tiles.append(acc)
    y_diag = jnp.concatenate(tiles, axis=1)

    e_exp = _expand_heads(jnp.exp(csc), onehot)
    y_off = jnp.dot(cb, st.astype(BF16), preferred_element_type=F32) * e_exp

    cs_last = csc[q - 1:q, :]
    te = _expand_heads(jnp.exp(cs_last - csc) * dtc, onehot)
    xw = (x * te).astype(BF16)
    d_st = jnp.dot(bm.T.astype(BF16), xw, preferred_element_type=F32)
    st_new = st * e_exp[q - 1:q, :] + d_st
    y = y_diag + y_off + dsk * x
    return _gate_norm(y, z, ng), st_new


def _causal_conv4_silu(x, prev8, w, bias):
    acc = w[3:4, :] * x + bias
    for k in (1, 2, 3):
        acc = acc + w[3 - k:4 - k, :] * _shifted_rows(x, prev8, k)
    return _silu(acc)


def _ssd_mix_kernel(x_ref, b_ref, c_ref, z_ref, wx_ref, wb_ref, wc_ref, bx_ref, bb_ref, bc_ref,
                    dtc_ref, csc_ref, dtr_ref, csr_ref, dsk_ref, ng_ref,
                    s_ref, xt_ref, dtt_ref, bt_ref, ce_ref, dec_ref,
                    y_ref, sfin_ref, hist_ref, so_ref, yt_ref,
                    st_ref, px_ref, pb_ref, pc_ref):
    q = SSD_CHUNK
    c = pl.program_id(1)

    @pl.when(c == 0)
    def _():
        st_ref[...] = jnp.zeros_like(st_ref)
        px_ref[...] = jnp.zeros_like(px_ref)
        pb_ref[...] = jnp.zeros_like(pb_ref)
        pc_ref[...] = jnp.zeros_like(pc_ref)

    _ssd_step_body(s_ref, xt_ref, dtt_ref, bt_ref, ce_ref, dec_ref, so_ref, yt_ref)

    for gi in range(SSD_GROUPS):
        xs = slice(GROUP_W * gi, GROUP_W * (gi + 1))
        ns = slice(SSD_STATE * gi, SSD_STATE * (gi + 1))
        hs = slice(SSD_HPG * gi, SSD_HPG * (gi + 1))
        x = _causal_conv4_silu(x_ref[:, xs], px_ref[:, xs], wx_ref[:, xs], bx_ref[:, xs])
        bm = _causal_conv4_silu(b_ref[:, ns], pb_ref[:, ns], wb_ref[:, ns], bb_ref[:, ns])
        cm = _causal_conv4_silu(c_ref[:, ns], pc_ref[:, ns], wc_ref[:, ns], bc_ref[:, ns])
        y, st_new = _ssd_group_chunk(x, bm, cm, z_ref[:, xs],
                                     dtc_ref[:, hs], csc_ref[:, hs], dtr_ref[hs, :], csr_ref[hs, :],
                                     dsk_ref[gi], ng_ref[gi], st_ref[gi])
        y_ref[:, xs] = y.astype(y_ref.dtype)
        st_ref[gi] = st_new

    px_ref[...] = x_ref[q - 8:, :]
    pb_ref[...] = b_ref[q - 8:, :]
    pc_ref[...] = c_ref[q - 8:, :]

    @pl.when(c == pl.num_programs(1) - 1)
    def _():
        for gi in range(SSD_GROUPS):
            sfin_ref[GROUP_W * gi:GROUP_W * (gi + 1), :] = st_ref[gi].T
        hist_ref[:, 0:D_INNER] = x_ref[q - 3:, :]
        hist_ref[:, D_INNER:D_INNER + BC_W] = b_ref[q - 3:, :]
        hist_ref[:, D_INNER + BC_W:] = c_ref[q - 3:, :]


def _ssd_mix(zx, conv_w, conv_b, dtc, csc, dtr, csr, dskip, ng, state, x_t, dt_t, b_tiled, c_exp, dec):
    q = SSD_CHUNK
    nc = SEQ // q
    assert BATCH * nc * SSD_STEP_SEQS == DEC_BATCH
    b_blk = D_INNER // BC_W
    small_c = pl.BlockSpec((q, SSD_HEADS), lambda b, c: (b * nc + c, 0))
    small_r = pl.BlockSpec((SSD_HEADS, q), lambda b, c: (0, b * nc + c))
    pspec = pl.BlockSpec((SSD_GROUPS, 1, GROUP_W), lambda b, c: (0, 0, 0))

    def rows(width, col_block):
        return pl.BlockSpec((q, width), lambda b, c: (b * nc + c, col_block))

    def par(nrows, width, col_block):
        return pl.BlockSpec((nrows, width), lambda b, c: (0, col_block))

    def seqs(*dims):
        return pl.BlockSpec((SSD_STEP_SEQS,) + dims, lambda b, c: (b * nc + c, 0, 0))

    tspec = seqs(LANES, N_TILES)
    sspec = seqs(D_INNER, SSD_STATE)
    return pl.pallas_call(
        _ssd_mix_kernel,
        grid=(BATCH, nc),
        in_specs=[rows(D_INNER, 1), rows(BC_W, 2 * b_blk), rows(BC_W, 2 * b_blk + 1), rows(D_INNER, 0),
                  par(4, D_INNER, 0), par(4, BC_W, b_blk), par(4, BC_W, b_blk + 1),
                  par(1, D_INNER, 0), par(1, BC_W, b_blk), par(1, BC_W, b_blk + 1),
                  small_c, small_c, small_r, small_r, pspec, pspec,
                  sspec, tspec, tspec, seqs(1, D_INNER), seqs(SSD_STATE, N_TILES), seqs(SSD_HEADS, SSD_STATE)],
        out_specs=[rows(D_INNER, 0),
                   pl.BlockSpec((D_INNER, SSD_STATE), lambda b, c: (b, 0)),
                   pl.BlockSpec((None, 3, SSD_CONV_DIM), lambda b, c: (b, 0, 0)),
                   sspec, tspec],
        out_shape=[jax.ShapeDtypeStruct((T_P, D_INNER), BF16),
                   jax.ShapeDtypeStruct((BATCH * D_INNER, SSD_STATE), F32),
                   jax.ShapeDtypeStruct((BATCH, 3, SSD_CONV_DIM), F32),
                   jax.ShapeDtypeStruct((DEC_BATCH, D_INNER, SSD_STATE), F32),
                   jax.ShapeDtypeStruct((DEC_BATCH, LANES, N_TILES), F32)],
        scratch_shapes=[pltpu.VMEM((SSD_GROUPS, SSD_STATE, GROUP_W), F32),
                        pltpu.VMEM((8, D_INNER), F32), pltpu.VMEM((8, BC_W), F32), pltpu.VMEM((8, BC_W), F32)],
        compiler_params=_cparams("arbitrary", "arbitrary"),
        name="ssd_mix",
    )(zx, zx, zx, zx, conv_w, conv_w, conv_w, conv_b, conv_b, conv_b, dtc, csc, dtr, csr, dskip, ng,
      state, x_t, dt_t, b_tiled, c_exp, dec)


def _ssd_step_body(s_ref, xt_ref, dtt_ref, bt_ref, ce_ref, dec_ref, so_ref, yt_ref):
    tile_of_lane = lax.broadcasted_iota(jnp.int32, (N_TILES, D_INNER), 1) // LANES
    row32 = lax.broadcasted_iota(jnp.int32, (N_TILES, D_INNER), 0)
    lane32 = lax.broadcasted_iota(jnp.int32, (SSD_STATE, N_TILES), 1)
    for b in range(SSD_STEP_SEQS):
        xdt = (xt_ref[b] * dtt_ref[b]).astype(BF16)
        b_big = jnp.where(tile_of_lane == row32, bt_ref[b], 0.0).astype(BF16)
        upd = jnp.dot(xdt, b_big, preferred_element_type=F32)

        ce = ce_ref[b]
        s_tiles, c_tiles = [], []
        for r in range(N_TILES):
            dtile = jnp.concatenate(
                [jnp.broadcast_to(dec_ref[b, 2 * r:2 * r + 1, :], (SSD_HEAD_DIM, SSD_STATE)),
                 jnp.broadcast_to(dec_ref[b, 2 * r + 1:2 * r + 2, :], (SSD_HEAD_DIM, SSD_STATE))], axis=0)
            s_n = s_ref[b, LANES * r:LANES * (r + 1), :] * dtile + upd[:, LANES * r:LANES * (r + 1)]
            so_ref[b, LANES * r:LANES * (r + 1), :] = s_n
            s_tiles.append(s_n.astype(BF16))
            c_tiles.append(jnp.where(lane32 == r, ce, 0.0).astype(BF16))
        s_big = jnp.concatenate(s_tiles, axis=1)
        c_big = jnp.concatenate(c_tiles, axis=0)
        yt_ref[b] = jnp.dot(s_big, c_big, preferred_element_type=F32)


def _gate_norm_kernel(y_ref, x_ref, z_ref, dsk_ref, ng_ref, o_ref):
    y = y_ref[...] + dsk_ref[...] * x_ref[...]
    o_ref[...] = _gate_norm(y, z_ref[...], ng_ref[...]).astype(o_ref.dtype)


def _gate_norm_sample(y_s, xbc_c_s, zx_s, dskip, ng):
    gspec = pl.BlockSpec((T_S, GROUP_W), lambda g: (0, g))
    pspec = pl.BlockSpec((None, 1, GROUP_W), lambda g: (g, 0, 0))
    return pl.pallas_call(
        _gate_norm_kernel,
        grid=(SSD_GROUPS,),
        in_specs=[gspec, gspec, gspec, pspec, pspec],
        out_specs=gspec,
        out_shape=jax.ShapeDtypeStruct((T_S, D_INNER), BF16),
        compiler_params=_cparams("arbitrary"),
        name="ssd_gate_norm",
    )(y_s, xbc_c_s, zx_s, dskip, ng)


def _xattn_kernel(qp_ref, kp_ref, vp_ref, qs_ref, kc_ref, vc_ref, op_ref, os_ref, *, bb):
    scale = MEM_HEAD_DIM ** -0.5
    for b in range(bb):
        for h in range(MEM_HEADS):
            kh = kc_ref[b, :, h, :]
            s = jnp.sum(kh * qs_ref[b, h:h + 1, :], axis=-1, keepdims=True) * scale
            m = jnp.max(s, axis=0, keepdims=True)
            e = jnp.exp(s - m)
            p = e / jnp.sum(e, axis=0, keepdims=True)
            os_ref[b, h:h + 1, :] = jnp.sum(p * vc_ref[b, :, h, :], axis=0, keepdims=True)

    for h in range(MEM_HEADS):
        sl = slice(MEM_HEAD_DIM * h, MEM_HEAD_DIM * (h + 1))
        kh = kp_ref[:, sl]
        vh = vp_ref[:, sl]
        s = lax.dot_general(qp_ref[:, sl], kh, _NT_DIMS, preferred_element_type=F32) * scale
        m = jnp.max(s, axis=-1, keepdims=True)
        e = jnp.exp(s - m)
        p = (e / jnp.sum(e, axis=-1, keepdims=True)).astype(BF16)
        op_ref[:, sl] = jnp.dot(p, vh, preferred_element_type=F32).astype(op_ref.dtype)


def _xattn(q_p, k_rows, v_rows, q_s, kc, vc, layer, *, bb=4):
    steps = DEC_BATCH // bb
    tq = T_P // steps
    tiles_per_seq = SEQ // tq
    kspec = pl.BlockSpec((None, MEM_LEN, D_MODEL), lambda i: (layer, i // tiles_per_seq, 0))
    cspec = pl.BlockSpec((None, bb, MEM_LEN, MEM_HEADS, MEM_HEAD_DIM), lambda i: (layer, i, 0, 0, 0))
    qp_spec = pl.BlockSpec((tq, D_MODEL), lambda i: (i, 0))
    qs_spec = pl.BlockSpec((bb, MEM_HEADS, MEM_HEAD_DIM), lambda i: (i, 0, 0))
    return pl.pallas_call(
        functools.partial(_xattn_kernel, bb=bb),
        grid=(steps,),
        in_specs=[qp_spec, kspec, kspec, qs_spec, cspec, cspec],
        out_specs=[qp_spec, qs_spec],
        out_shape=[jax.ShapeDtypeStruct((T_P, D_MODEL), BF16),
                   jax.ShapeDtypeStruct((DEC_BATCH, MEM_HEADS, MEM_HEAD_DIM), F32)],
        compiler_params=_cparams("arbitrary"),
        name="xattn",
    )(q_p, k_rows, v_rows, q_s, kc, vc)


def _mem_attn_ffn_block(hp, hs, hq, i, norm_ffn, xa_w_q, xa_w_o, k_p, v_p, kc, vc, w_gate, w_up, w_down):
    q_p, q_s = _matmul2(hq[0], hq[1], [xa_w_q], i, tn=1024, tm=1024, out_dtype=BF16)
    o_p, o_s = _xattn(q_p, k_p, v_p, q_s.astype(F32).reshape(DEC_BATCH, MEM_HEADS, MEM_HEAD_DIM), kc, vc, i)
    o_s = o_s.reshape(DEC_BATCH, D_MODEL).astype(BF16)
    hp, hs, hf_p, hf_s = _matmul_res_norm(o_p, o_s, xa_w_o, i, (hp, hs), norm_ffn[i])
    a_p, a_s = _matmul2(hf_p, hf_s, [w_gate, w_up], i, tn=512, tm=1024, out_dtype=BF16, epilogue=_swiglu_epilogue)
    return _matmul2(a_p, a_s, [w_down], i, tn=1024, tm=256, res=(hp, hs), w_buffers=1)


def _short_conv_block(hp, hs, a, norm_g, norm_next, cache_sc, sc_w_in, sc_w_conv, sc_w_out):
    hn_p, hn_s = _rmsnorm2(hp, hs, norm_g, BF16)
    h0, h1 = cache_sc[a, :, 0, :], cache_sc[a, :, 1, :]
    y_p, y_s, u_s, hist_p = _sc_in_conv(hn_p, hn_s, sc_w_in, a, sc_w_conv[a], h0, h1)
    hp, hs, hq_p, hq_s = _matmul_res_norm(y_p, y_s, sc_w_out, a, (hp, hs), norm_next)
    return hp, hs, (hq_p, hq_s), hist_p, jnp.stack([h1, u_s], axis=1)


def _ssd_block(hp, hs, j, norm_g, conv_hist, state, ssd_w_in, conv_w, conv_b, dt_bias, a_log, d_skip, norm_ssd, w_out):
    hn_p, hn_s = _rmsnorm2(hp, hs, norm_g, BF16)
    w_in_t = jnp.swapaxes(ssd_w_in, 1, 2)
    zx_p, zx_s = _matmul2(hn_p, hn_s, [w_in_t], j, tn=1024, tm=1024, n_out=D_INNER + SSD_CONV_DIM,
                          w_transposed=True)

    dt_p, cs_p, dt_s, dec_s = _dt_prep(hn_p, hn_s, w_in_t, j, dt_bias[j], a_log[j])

    cw, cbias = conv_w[j], conv_b[j].reshape(1, SSD_CONV_DIM)
    ch = conv_hist[j]
    xbc_c_s, raw_s = _ssdconv_sample(zx_s, ch[:, 0, :], ch[:, 1, :], ch[:, 2, :], cw, cbias)
    chist_s = jnp.stack([ch[:, 1, :], ch[:, 2, :], raw_s], axis=1)

    dskip = jnp.repeat(d_skip[j], SSD_HEAD_DIM).reshape(SSD_GROUPS, 1, GROUP_W)
    ng = norm_ssd[j].reshape(SSD_GROUPS, 1, GROUP_W)

    def tile_major(v):
        return jnp.transpose(v.reshape(DEC_BATCH, N_TILES, LANES), (0, 2, 1))

    tiles_per_group = GROUP_W // LANES
    x_t = tile_major(xbc_c_s[:, :D_INNER])
    dt_t = tile_major(jnp.repeat(dt_s[:, :SSD_HEADS], SSD_HEAD_DIM, axis=1))
    b_g = xbc_c_s[:, D_INNER:D_INNER + BC_W].reshape(DEC_BATCH, SSD_GROUPS, SSD_STATE)
    c_g = xbc_c_s[:, D_INNER + BC_W:].reshape(DEC_BATCH, SSD_GROUPS, SSD_STATE)
    b_tiled = jnp.repeat(b_g, tiles_per_group, axis=1).reshape(DEC_BATCH, 1, D_INNER)
    c_exp = jnp.transpose(jnp.repeat(c_g, tiles_per_group, axis=1), (0, 2, 1))
    dec_b = jnp.broadcast_to(dec_s[:, :SSD_HEADS, None], (DEC_BATCH, SSD_HEADS, SSD_STATE))

    g_p, sfin, chist_p, s_new, y_t = _ssd_mix(
        zx_p, cw, cbias, dt_p, cs_p, dt_p.T, cs_p.T, dskip, ng,
        state[j].reshape(DEC_BATCH, D_INNER, SSD_STATE), x_t, dt_t, b_tiled, c_exp, dec_b)
    ss_p = sfin.reshape(BATCH, SSD_HEADS, SSD_HEAD_DIM, SSD_STATE)
    y_s = jnp.transpose(y_t, (0, 2, 1)).reshape(DEC_BATCH, D_INNER)
    g_s = _gate_norm_sample(y_s, xbc_c_s, zx_s, dskip, ng)
    ss_s = s_new.reshape(DEC_BATCH, SSD_HEADS, SSD_HEAD_DIM, SSD_STATE)

    hp, hs = _matmul2(g_p, g_s, [w_out], j, tn=1024, tm=512, res=(hp, hs), w_buffers=1)
    return hp, hs, chist_p, chist_s, ss_p, ss_s


def kernel(x_prompt, x_sample, mem_prompt, cache_sc, state_ssd_conv, state_ssd, cache_mem_k, cache_mem_v, norm_mix, norm_mem_q, norm_mem_kv, norm_ffn, norm_final, sc_w_in, sc_w_conv, sc_w_out, ssd_w_in, ssd_conv_w, ssd_conv_b, ssd_dt_bias, ssd_a_log, ssd_d, ssd_norm, ssd_w_out, xa_w_q, xa_w_k, xa_w_v, xa_w_o, ffn_w_gate, ffn_w_up, ffn_w_down):
    depth = norm_mix.shape[0]
    hp = x_prompt.reshape(T_P, D_MODEL)
    hs = x_sample.reshape(T_S, D_MODEL)

    mem = mem_prompt.reshape(BATCH * MEM_LEN, D_MODEL)
    k_rows, mem_k_p = _mem_proj(mem, norm_mem_kv, xa_w_k)
    v_rows, mem_v_p = _mem_proj(mem, norm_mem_kv, xa_w_v)

    sc_p, sc_s, sconv_p, sconv_s, ss_p, ss_s = [], [], [], [], [], []
    for i in range(depth):
        if i % 2 == 0:
            hp, hs, hq, a_p, a_s = _short_conv_block(hp, hs, i // 2, norm_mix[i], norm_mem_q[i], cache_sc,
                                                     sc_w_in, sc_w_conv, sc_w_out)
            sc_p.append(a_p)
            sc_s.append(a_s)
        else:
            hp, hs, c_p, c_s, s_p, s_s = _ssd_block(hp, hs, i // 2, norm_mix[i], state_ssd_conv, state_ssd, ssd_w_in,
                                                    ssd_conv_w, ssd_conv_b, ssd_dt_bias, ssd_a_log, ssd_d, ssd_norm,
                                                    ssd_w_out)
            sconv_p.append(c_p)
            sconv_s.append(c_s)
            ss_p.append(s_p)
            ss_s.append(s_s)
            hq = _rmsnorm2(hp, hs, norm_mem_q[i], BF16)
        hp, hs = _mem_attn_ffn_block(hp, hs, hq, i, norm_ffn, xa_w_q, xa_w_o, k_rows, v_rows,
                                     cache_mem_k, cache_mem_v, ffn_w_gate, ffn_w_up, ffn_w_down)

    y_p, y_s = _rmsnorm2(hp, hs, norm_final, F32)

    return (y_p.reshape(BATCH, SEQ, D_MODEL), y_s.reshape(DEC_BATCH, 1, D_MODEL),
            jnp.stack(sc_p), jnp.stack(sc_s), jnp.stack(sconv_p), jnp.stack(sconv_s),
            jnp.stack(ss_p), jnp.stack(ss_s), mem_k_p, mem_v_p)
```
